```python
import math, functools
import jax, jax.numpy as jnp
from jax import lax
import numpy as np

D_MODEL = 1024
BATCH = 2
SEQ = 8192
DEPTH = 1
DEC_BATCH = 32
DEC_SEQ = 4
PAST_LEN = 16384
PAGE_SIZE = 128

N_HEADS = 8
HEAD_DIM = 64
ATT_WIDTH = N_HEADS * HEAD_DIM
CONV_GROUPS = 8
CONV_GROUP_CH = 64
CONV_CH = CONV_GROUPS * CONV_GROUP_CH
CONV_WIDTH = 3
D_FF = 4 * D_MODEL
Q_BLOCK = 128
EPS = 1e-6
N_MOD = 6
SB_SCALE = 1.0 / math.sqrt(HEAD_DIM)
SB_BIAS_INIT = -8.0
IN_SPLITS = (ATT_WIDTH,) * 3 + (CONV_CH,) * 3 + (D_MODEL,) * 2
IN_WIDTH = sum(IN_SPLITS)
IN_OFFSETS = np.cumsum(IN_SPLITS)[:-1].tolist()

kernel_name = "stickbreak_shortconv_gated_hybrid_step"


def rms_norm(x, g):
    x32 = x.astype(jnp.float32)
    y = x32 * lax.rsqrt(jnp.mean(x32 * x32, axis=-1, keepdims=True) + EPS)
    return (y * g.astype(jnp.float32)).astype(x.dtype)


def ada_mod(c, w_ada, b_ada):
    m = jax.nn.silu(c) @ w_ada + b_ada
    return [t[:, None, :] for t in jnp.split(m, N_MOD, axis=-1)]


def project_in(h, w_in, q_norm_g, k_norm_g):
    b, t, _ = h.shape
    parts = jnp.split(h @ w_in, IN_OFFSETS, axis=-1)
    q = rms_norm(parts[0].reshape(b, t, N_HEADS, HEAD_DIM), q_norm_g)
    k = rms_norm(parts[1].reshape(b, t, N_HEADS, HEAD_DIM), k_norm_g)
    v = parts[2].reshape(b, t, N_HEADS, HEAD_DIM)
    return q, k, v, parts[3], parts[4], parts[5], parts[6], parts[7]


def stick_breaking(z, valid):
    log_beta = jax.nn.log_sigmoid(z)
    log_keep = jnp.where(valid, jax.nn.log_sigmoid(-z), 0.0)
    suffix = lax.cumsum(log_keep, axis=z.ndim - 1, reverse=True) - log_keep
    return jnp.where(valid, jnp.exp(log_beta + suffix), 0.0)


def sb_logits(q, k, sb_bias):
    z = jnp.einsum('bqhd,bkhd->bhqk', q, k).astype(jnp.float32) * SB_SCALE
    return z + sb_bias.astype(jnp.float32)[None, :, None, None]


def sb_attention_prompt(q, k, v, sb_bias):
    b, t, h, d = q.shape
    key_pos = jnp.arange(t)

    def block(i):
        start = i * Q_BLOCK
        qb = lax.dynamic_slice_in_dim(q, start, Q_BLOCK, axis=1)
        z = sb_logits(qb, k, sb_bias)
        q_pos = start + jnp.arange(Q_BLOCK)
        valid = key_pos[None, :] < q_pos[:, None]
        a = stick_breaking(z, valid).astype(v.dtype)
        return jnp.einsum('bhqk,bkhd->bqhd', a, v)

    o = lax.map(block, jnp.arange(t // Q_BLOCK))
    return jnp.moveaxis(o, 0, 1).reshape(b, t, h, d)


def sb_attention_sample(q, k_new, v_new, sb_bias, k_past, v_past):
    p = k_past.shape[1]
    t = q.shape[1]
    z = jnp.concatenate([sb_logits(q, k_past, sb_bias), sb_logits(q, k_new, sb_bias)], axis=-1)
    key_pos = jnp.arange(p + t)
    q_pos = p + jnp.arange(t)
    valid = key_pos[None, :] < q_pos[:, None]
    a = stick_breaking(z, valid).astype(v_new.dtype)
    return (jnp.einsum('bhqk,bkhd->bqhd', a[..., :p], v_past)
            + jnp.einsum('bhqk,bkhd->bqhd', a[..., p:], v_new))


def short_conv(u, prev, w, bias):
    t = u.shape[1]
    win = jnp.concatenate([prev.astype(u.dtype), u], axis=1)
    y = bias + sum(w[j] * win[:, j:j + t] for j in range(CONV_WIDTH))
    return y, win[:, -(CONV_WIDTH - 1):]


def merge_branches(o_att, y_conv, ga, gb, w_att_out, w_conv_out, w_o):
    b, t = y_conv.shape[:2]
    ya = o_att.reshape(b, t, ATT_WIDTH) @ w_att_out
    yb = y_conv @ w_conv_out
    return (jax.nn.sigmoid(ga) * ya + jax.nn.sigmoid(gb) * yb) @ w_o


def sqrelu_mlp(h, w1, w2):
    return jnp.square(jax.nn.relu(h @ w1)) @ w2


def decoder_layer(x, c, attend, conv_prev, rms_g1, rms_g2, w_ada, b_ada, w_in, q_norm_g, k_norm_g, sb_bias,
                  conv_w, conv_b, w_att_out, w_conv_out, w_o, w_mlp1, w_mlp2):
    sh1, sc1, gt1, sh2, sc2, gt2 = ada_mod(c, w_ada, b_ada)
    h = rms_norm(x, rms_g1) * (1 + sc1) + sh1
    q, k, v, hc, bgate, cgate, ga, gb = project_in(h, w_in, q_norm_g, k_norm_g)
    o_att = attend(q, k, v, sb_bias)
    yc, conv_state = short_conv(cgate * hc, conv_prev, conv_w, conv_b)
    x = x + gt1 * merge_branches(o_att, bgate * yc, ga, gb, w_att_out, w_conv_out, w_o)
    h2 = rms_norm(x, rms_g2) * (1 + sc2) + sh2
    x = x + gt2 * sqrelu_mlp(h2, w_mlp1, w_mlp2)
    return x, k, v, conv_state


def setup_inputs(seed: int = 0) -> dict:
    key = jax.random.key(seed)
    ks = jax.random.split(key, 24)
    n_pages = PAST_LEN // PAGE_SIZE
    n_used = DEC_BATCH * n_pages
    n_phys = n_used + max(1, n_used // 4)
    f32 = jnp.float32
    nrm = lambda k, s, scale: jax.random.normal(k, s, f32) * scale
    page_table = jax.random.permutation(ks[7], n_phys)[:n_used].reshape(DEC_BATCH, n_pages).astype(jnp.int32)
    return {
        "x_prompt": nrm(ks[0], (BATCH, SEQ, D_MODEL), 1.0),
        "x_sample": nrm(ks[1], (DEC_BATCH, DEC_SEQ, D_MODEL), 1.0),
        "c_prompt": nrm(ks[2], (BATCH, D_MODEL), 1.0),
        "c_sample": nrm(ks[3], (DEC_BATCH, D_MODEL), 1.0),
        "cache_k": nrm(ks[4], (DEPTH, n_phys, PAGE_SIZE, N_HEADS, HEAD_DIM), 1.0),
        "cache_v": nrm(ks[5], (DEPTH, n_phys, PAGE_SIZE, N_HEADS, HEAD_DIM), 1.0),
        "state_conv": nrm(ks[6], (DEPTH, DEC_BATCH, CONV_WIDTH - 1, CONV_CH), 1.0),
        "page_table": page_table,
        "rms_g1": 1.0 + nrm(ks[8], (DEPTH, D_MODEL), 0.02),
        "rms_g2": 1.0 + nrm(ks[9], (DEPTH, D_MODEL), 0.02),
        "w_ada": nrm(ks[10], (DEPTH, D_MODEL, N_MOD * D_MODEL), 0.2 * D_MODEL ** -0.5),
        "b_ada": nrm(ks[11], (DEPTH, N_MOD * D_MODEL), 0.01),
        "w_in": nrm(ks[12], (DEPTH, D_MODEL, IN_WIDTH), D_MODEL ** -0.5),
        "q_norm_g": 1.0 + nrm(ks[13], (DEPTH, HEAD_DIM), 0.02),
        "k_norm_g": 1.0 + nrm(ks[14], (DEPTH, HEAD_DIM), 0.02),
        "sb_bias": SB_BIAS_INIT + nrm(ks[22], (DEPTH, N_HEADS), 0.1),
        "conv_w": nrm(ks[15], (DEPTH, CONV_WIDTH, CONV_CH), CONV_WIDTH ** -0.5),
        "conv_b": nrm(ks[16], (DEPTH, CONV_CH), 0.01),
        "w_att_out": nrm(ks[17], (DEPTH, ATT_WIDTH, D_MODEL), ATT_WIDTH ** -0.5),
        "w_conv_out": nrm(ks[18], (DEPTH, CONV_CH, D_MODEL), CONV_CH ** -0.5),
        "w_o": nrm(ks[19], (DEPTH, D_MODEL, D_MODEL), D_MODEL ** -0.5),
        "w_mlp1": nrm(ks[20], (DEPTH, D_MODEL, D_FF), D_MODEL ** -0.5),
        "w_mlp2": nrm(ks[21], (DEPTH, D_FF, D_MODEL), D_FF ** -0.5),
    }


def reference(x_prompt, x_sample, c_prompt, c_sample, cache_k, cache_v, state_conv, page_table,
              rms_g1, rms_g2, w_ada, b_ada, w_in, q_norm_g, k_norm_g, sb_bias, conv_w, conv_b,
              w_att_out, w_conv_out, w_o, w_mlp1, w_mlp2):
    yp, ys = x_prompt, x_sample
    bp, bs = x_prompt.shape[0], x_sample.shape[0]
    kp_l, vp_l, cp_l, ks_l, vs_l, cs_l = [], [], [], [], [], []
    for l in range(DEPTH):
        lw = (rms_g1[l], rms_g2[l], w_ada[l], b_ada[l], w_in[l], q_norm_g[l], k_norm_g[l], sb_bias[l],
              conv_w[l], conv_b[l], w_att_out[l], w_conv_out[l], w_o[l], w_mlp1[l], w_mlp2[l])
        conv0 = jnp.zeros((bp, CONV_WIDTH - 1, CONV_CH), yp.dtype)
        yp, kp, vp, cp = decoder_layer(yp, c_prompt, sb_attention_prompt, conv0, *lw)
        k_past = cache_k[l][page_table].reshape(bs, -1, N_HEADS, HEAD_DIM)
        v_past = cache_v[l][page_table].reshape(bs, -1, N_HEADS, HEAD_DIM)
        attend = functools.partial(sb_attention_sample, k_past=k_past, v_past=v_past)
        ys, kn, vn, cn = decoder_layer(ys, c_sample, attend, state_conv[l], *lw)
        kp_l.append(kp); vp_l.append(vp); cp_l.append(cp)
        ks_l.append(kn); vs_l.append(vn); cs_l.append(cn)
    return (yp, ys, jnp.stack(kp_l), jnp.stack(vp_l), jnp.stack(cp_l),
            jnp.stack(ks_l), jnp.stack(vs_l), jnp.stack(cs_l))
```

```python
import functools
import math

import jax
import jax.numpy as jnp
from jax import lax
from jax.experimental import pallas as pl
from jax.experimental.pallas import tpu as pltpu

D_MODEL = 1024
N_HEADS = 8
HEAD_DIM = 64
ATT_WIDTH = N_HEADS * HEAD_DIM
CONV_CH = 512
CONV_WIDTH = 3
D_FF = 4 * D_MODEL
N_MOD = 6
PAGE_SIZE = 128
EPS = 1e-6
SB_SCALE = 1.0 / math.sqrt(HEAD_DIM)
IN_WIDTH = 3 * ATT_WIDTH + 3 * CONV_CH + 2 * D_MODEL

V7X_LANES = 128
V7X_SUBLANES = 8
V7X_MXU_DIM = 256
V7X_VMEM_BYTES = 64 * 1024 * 1024

ROW_TILE = 512
Q_TILE = 512
K_TILE = V7X_MXU_DIM
PAGES_PER_STEP = 16
HIST_ROWS = V7X_SUBLANES

F32 = jnp.float32
BF16 = jnp.bfloat16


def _vmem_limit(nbytes):
    return int(min(nbytes + (8 << 20), V7X_VMEM_BYTES - (6 << 20)))


def _resident(shape, index_map):
    return pl.BlockSpec(shape, index_map, pipeline_mode=pl.Buffered(1))


def _ada_kernel(c_ref, w_ref, b_ref, o_ref):
    c = c_ref[...]
    s = (c * jax.nn.sigmoid(c)).astype(BF16)
    o_ref[...] = jnp.dot(s, w_ref[...], preferred_element_type=F32) + b_ref[...]


def _ada(c_all, w_ada, b_ada):
    n, d = c_all.shape
    width = w_ada.shape[1]
    tn = 1536
    return pl.pallas_call(
        _ada_kernel,
        grid=(width // tn,),
        in_specs=[pl.BlockSpec((n, d), lambda j: (0, 0)),
                  pl.BlockSpec((d, tn), lambda j: (0, j)),
                  pl.BlockSpec((1, tn), lambda j: (0, j))],
        out_specs=pl.BlockSpec((n, tn), lambda j: (0, j)),
        out_shape=jax.ShapeDtypeStruct((n, width), F32),
        compiler_params=pltpu.CompilerParams(
            dimension_semantics=("arbitrary",),
            vmem_limit_bytes=_vmem_limit(2 * d * tn * 4)),
        name="ada",
    )(c_all, w_ada, b_ada.reshape(1, width))


def _head_rms(p, gmat_ref, gain):
    sq = (p * p).astype(BF16)
    parts = []
    for c in range(ATT_WIDTH // V7X_MXU_DIM):
        sl = slice(c * V7X_MXU_DIM, (c + 1) * V7X_MXU_DIM)
        parts.append(jnp.dot(sq[:, sl], gmat_ref[...], preferred_element_type=F32))
    ms = jnp.concatenate(parts, axis=-1)
    return p * lax.rsqrt(ms + EPS) * gain


def _proj_kernel(*refs, tm, sample_mode):
    if sample_mode:
        (x_ref, sh_ref, sc_ref, g1_ref, w_ref, qg_ref, kg_ref, gmat_ref, cw_ref, cb_ref, h1_ref, h2_ref,
         q_out, k_out, kb_out, v_out, vb_out, yc_out, sga_out, sgb_out, u_out, conv_buf) = refs
    else:
        (x_ref, sh_ref, sc_ref, g1_ref, w_ref, qg_ref, kg_ref, gmat_ref, cw_ref, cb_ref,
         q_out, k_out, kb_out, v_out, vb_out, yc_out, sga_out, sgb_out, cs_out, conv_buf) = refs

    x = x_ref[0]
    ms = jnp.mean(x * x, axis=-1, keepdims=True)
    h = x * lax.rsqrt(ms + EPS) * g1_ref[...]
    h = (h * (1.0 + sc_ref[0]) + sh_ref[0]).astype(BF16)

    def part(lo, width):
        return jnp.dot(h, w_ref[:, lo:lo + width], preferred_element_type=F32)

    a = ATT_WIDTH
    qn = _head_rms(part(0, a), gmat_ref, qg_ref[...])
    q_out[0] = (qn * SB_SCALE).astype(BF16)
    kn = _head_rms(part(a, a), gmat_ref, kg_ref[...])
    k_out[0] = kn
    kb_out[0] = kn.astype(BF16)
    v = part(2 * a, a)
    v_out[0] = v
    vb_out[0] = v.astype(BF16)

    c0 = 3 * a
    hc = part(c0, CONV_CH)
    bgate = part(c0 + CONV_CH, CONV_CH)
    cgate = part(c0 + 2 * CONV_CH, CONV_CH)
    u = cgate * hc

    conv_buf[HIST_ROWS:HIST_ROWS + tm, :] = u
    if sample_mode:
        u_out[0] = u
        conv_buf[0:HIST_ROWS, :] = jnp.zeros((HIST_ROWS, CONV_CH), F32)
    else:
        @pl.when(pl.program_id(1) == 0)
        def _():
            conv_buf[0:HIST_ROWS, :] = jnp.zeros((HIST_ROWS, CONV_CH), F32)

    u1 = conv_buf[HIST_ROWS - 1:HIST_ROWS - 1 + tm, :]
    u2 = conv_buf[HIST_ROWS - 2:HIST_ROWS - 2 + tm, :]
    if sample_mode:
        u1 = jnp.where(h1_ref[0] != 0, h1_ref[1], u1)
        u2 = jnp.where(h2_ref[0] != 0, h2_ref[1], u2)
    y = cb_ref[...] + cw_ref[0:1, :] * u2 + cw_ref[1:2, :] * u1 + cw_ref[2:3, :] * u
    yc_out[0] = (bgate * y).astype(BF16)
    if not sample_mode:
        conv_buf[0:HIST_ROWS, :] = u[tm - HIST_ROWS:, :]

        @pl.when(pl.program_id(1) == pl.num_programs(1) - 1)
        def _():
            cs_out[0] = u[tm - (CONV_WIDTH - 1):, :]

    g0 = c0 + 3 * CONV_CH
    sga_out[0] = jax.nn.sigmoid(part(g0, D_MODEL)).astype(BF16)
    sgb_out[0] = jax.nn.sigmoid(part(g0 + D_MODEL, D_MODEL)).astype(BF16)


def _proj(x, sh, sc, g1, w_in_bf, qg, kg, gmat, conv_w, conv_b, hist=None):
    nb, t, d = x.shape
    sample_mode = hist is not None
    tm = t if sample_mode else ROW_TILE
    nt = t // tm
    mod_rows = sh.shape[1]
    mod_block = (1, tm, d) if mod_rows == t else (1, 1, d)
    mod_map = (lambda b, i: (b, i, 0)) if mod_rows == t else (lambda b, i: (b, 0, 0))
    row = lambda w: pl.BlockSpec((1, tm, w), lambda b, i: (b, i, 0))
    const = lambda shape: _resident(shape, lambda b, i: (0,) * len(shape))

    in_specs = [row(d), pl.BlockSpec(mod_block, mod_map), pl.BlockSpec(mod_block, mod_map),
                const((1, d)), const((d, IN_WIDTH)), const((1, ATT_WIDTH)), const((1, ATT_WIDTH)),
                const((V7X_MXU_DIM, V7X_MXU_DIM)), const((CONV_WIDTH, CONV_CH)), const((1, CONV_CH))]
    args = [x, sh, sc, g1, w_in_bf, qg, kg, gmat, conv_w, conv_b]
    out_shape = [jax.ShapeDtypeStruct((nb, t, ATT_WIDTH), BF16),
                 jax.ShapeDtypeStruct((nb, t, ATT_WIDTH), F32),
                 jax.ShapeDtypeStruct((nb, t, ATT_WIDTH), BF16),
                 jax.ShapeDtypeStruct((nb, t, ATT_WIDTH), F32),
                 jax.ShapeDtypeStruct((nb, t, ATT_WIDTH), BF16),
                 jax.ShapeDtypeStruct((nb, t, CONV_CH), BF16),
                 jax.ShapeDtypeStruct((nb, t, D_MODEL), BF16),
                 jax.ShapeDtypeStruct((nb, t, D_MODEL), BF16)]
    out_specs = [row(ATT_WIDTH)] * 5 + [row(CONV_CH), row(D_MODEL), row(D_MODEL)]
    if sample_mode:
        in_specs += [const((2, t, CONV_CH)), const((2, t, CONV_CH))]
        args += list(hist)
        out_shape.append(jax.ShapeDtypeStruct((nb, t, CONV_CH), F32))
        out_specs.append(row(CONV_CH))
    else:
        out_shape.append(jax.ShapeDtypeStruct((nb, CONV_WIDTH - 1, CONV_CH), F32))
        out_specs.append(pl.BlockSpec((1, CONV_WIDTH - 1, CONV_CH), lambda b, i: (b, 0, 0)))

    est = (d * IN_WIDTH * 2 + 2 * tm * d * 4 + 2 * tm * (ATT_WIDTH * 14 + CONV_CH * 6 + D_MODEL * 4)
           + 6 * tm * D_MODEL * 4)
    return pl.pallas_call(
        functools.partial(_proj_kernel, tm=tm, sample_mode=sample_mode),
        grid=(nb, nt),
        in_specs=in_specs,
        out_specs=out_specs,
        out_shape=out_shape,
        scratch_shapes=[pltpu.VMEM((tm + HIST_ROWS, CONV_CH), F32)],
        compiler_params=pltpu.CompilerParams(
            dimension_semantics=("arbitrary", "arbitrary"),
            vmem_limit_bytes=_vmem_limit(est)),
        name="proj_sample" if sample_mode else "proj_prompt",
    )(*args)


def _softplus(z):
    return jnp.maximum(z, 0.0) + jnp.log(1.0 + jnp.exp(-jnp.abs(z)))


def _attn_prompt_kernel(bias_ref, q_ref, k_ref, v_ref, u_ref, o_ref):
    hp = pl.program_id(1)
    i = pl.program_id(2)
    tq = q_ref.shape[1]
    pair = q_ref.shape[2]
    q2 = q_ref[0]
    lane = lax.broadcasted_iota(jnp.int32, (tq, pair), 1)
    row = lax.broadcasted_iota(jnp.int32, (tq, K_TILE), 0)
    col = lax.broadcasted_iota(jnp.int32, (tq, K_TILE), 1)
    tiles_per_q = tq // K_TILE

    accs = []
    for hh in range(pair // HEAD_DIM):
        in_head = (lane >= hh * HEAD_DIM) & (lane < (hh + 1) * HEAD_DIM)
        qm = jnp.where(in_head, q2, jnp.zeros_like(q2))
        bias = bias_ref[hp * (pair // HEAD_DIM) + hh]

        def tile(j, carry, acc, diag_off=None):
            k0 = pl.multiple_of(j * K_TILE, K_TILE)
            kt = k_ref[0, pl.ds(k0, K_TILE), :]
            vt = v_ref[0, pl.ds(k0, K_TILE), :]
            z = lax.dot_general(qm, kt, (((1,), (1,)), ((), ())), preferred_element_type=F32) + bias
            sp = _softplus(z)
            if diag_off is not None:
                valid = (row - col) > diag_off
                sp = jnp.where(valid, sp, 0.0)
            p = jnp.dot(sp.astype(BF16), u_ref[...], preferred_element_type=F32)
            a = jnp.exp(z + p + carry)
            if diag_off is not None:
                a = jnp.where(valid, a, 0.0)
            carry = carry + p[:, 0:1]
            acc = acc + jnp.dot(a.astype(BF16), vt, preferred_element_type=F32)
            return carry, acc

        carry = jnp.zeros((tq, 1), F32)
        acc = jnp.zeros((tq, pair), F32)
        for d in reversed(range(tiles_per_q)):
            carry, acc = tile(i * tiles_per_q + d, carry, acc, diag_off=d * K_TILE)
        n_full = i * tiles_per_q

        def body(n, c):
            return tile(n_full - 1 - n, c[0], c[1])

        carry, acc = lax.fori_loop(0, n_full, body, (carry, acc))
        accs.append((in_head, acc))

    out = jnp.zeros((tq, pair), F32)
    for in_head, acc in accs:
        out = jnp.where(in_head, acc, out)
    o_ref[0] = out.astype(o_ref.dtype)


def _attn_prompt(q_bf, k_bf, v_bf, bias, umat):
    nb, t, _ = q_bf.shape
    pair = V7X_LANES
    n_pairs = ATT_WIDTH // pair
    tq = Q_TILE
    est = 2 * 2 * t * pair * 2 + 4 * tq * pair * 2 + 12 * tq * K_TILE * 4
    return pl.pallas_call(
        _attn_prompt_kernel,
        grid=(nb, n_pairs, t // tq),
        in_specs=[pl.BlockSpec(memory_space=pltpu.SMEM),
                  pl.BlockSpec((1, tq, pair), lambda b, h, i: (b, i, h)),
                  pl.BlockSpec((1, t, pair), lambda b, h, i: (b, 0, h)),
                  pl.BlockSpec((1, t, pair), lambda b, h, i: (b, 0, h)),
                  _resident((K_TILE, K_TILE), lambda b, h, i: (0, 0))],
        out_specs=pl.BlockSpec((1, tq, pair), lambda b, h, i: (b, i, h)),
        out_shape=jax.ShapeDtypeStruct((nb, t, ATT_WIDTH), BF16),
        compiler_params=pltpu.CompilerParams(
            dimension_semantics=("arbitrary", "arbitrary", "arbitrary"),
            vmem_limit_bytes=_vmem_limit(est)),
        name="attn_prompt",
    )(bias, q_bf, k_bf, v_bf, umat)


def _attn_sample_kernel(pt_ref, qbd_ref, brow_ref, u_ref, hmask_ref, kn_ref, vn_ref, *rest, n_pages, n_q):
    del pt_ref
    k_refs = rest[:n_pages]
    v_refs = rest[n_pages:2 * n_pages]
    o_ref, acc_ref, carry_ref = rest[2 * n_pages:]
    c = pl.program_id(1)
    n_cols = n_q * N_HEADS
    qbd = qbd_ref[0]
    brow = brow_ref[...]

    def page(kp, vp, carry, valid=None):
        z = jnp.dot(kp, qbd, preferred_element_type=F32) + brow
        sp = _softplus(z)
        if valid is not None:
            sp = jnp.where(valid, sp, 0.0)
        p = jnp.dot(u_ref[...], sp.astype(BF16), preferred_element_type=F32)
        a = jnp.exp(z + p + carry)
        if valid is not None:
            a = jnp.where(valid, a, 0.0)
        at = a.T[0:n_cols, :].astype(BF16)
        contrib = jnp.dot(at, vp, preferred_element_type=F32)
        return carry + p[0:1, :], contrib

    @pl.when(c == 0)
    def _():
        key = lax.broadcasted_iota(jnp.int32, (PAGE_SIZE, V7X_LANES), 0)
        qry = lax.broadcasted_iota(jnp.int32, (PAGE_SIZE, V7X_LANES), 1) // N_HEADS
        carry, contrib = page(kn_ref[0], vn_ref[0], jnp.zeros((1, V7X_LANES), F32), valid=key < qry)
        carry_ref[...] = jnp.broadcast_to(carry, carry_ref.shape)
        acc_ref[...] = contrib

    carry = carry_ref[0:1, :]
    acc = acc_ref[...]
    for g in reversed(range(n_pages)):
        carry, contrib = page(k_refs[g][0], v_refs[g][0], carry)
        acc = acc + contrib
    carry_ref[...] = jnp.broadcast_to(carry, carry_ref.shape)
    acc_ref[...] = acc

    @pl.when(c == pl.num_programs(1) - 1)
    def _():
        hm = hmask_ref[...]
        for t in range(n_q):
            blk = acc[t * N_HEADS:(t + 1) * N_HEADS, :] * hm
            o_ref[0, t:t + 1, :] = jnp.sum(blk, axis=0, keepdims=True)


def _attn_sample(page_table, qbd, brow, umat, hmask, k_new_pad, v_new_pad, cache_k, cache_v, n_q):
    nb, n_tab = page_table.shape
    g = PAGES_PER_STEP
    n_chunks = n_tab // g
    page_block = (1, PAGE_SIZE, ATT_WIDTH)

    def page_spec(slot):
        return pl.BlockSpec(page_block, lambda b, c, pt: (pt[b, (n_chunks - 1 - c) * g + slot], 0, 0))

    per_seq = lambda shape: pl.BlockSpec(shape, lambda b, c, pt: (b, 0, 0))
    const = lambda shape: pl.BlockSpec(shape, lambda b, c, pt: (0,) * len(shape))
    in_specs = ([per_seq((1, ATT_WIDTH, V7X_LANES)), const((1, V7X_LANES)), const((PAGE_SIZE, PAGE_SIZE)),
                 const((N_HEADS, ATT_WIDTH)), per_seq(page_block), per_seq(page_block)]
                + [page_spec(s) for s in range(g)] + [page_spec(s) for s in range(g)])
    grid_spec = pltpu.PrefetchScalarGridSpec(
        num_scalar_prefetch=1,
        grid=(nb, n_chunks),
        in_specs=in_specs,
        out_specs=pl.BlockSpec((1, n_q, ATT_WIDTH), lambda b, c, pt: (b, 0, 0)),
        scratch_shapes=[pltpu.VMEM((n_q * N_HEADS, ATT_WIDTH), F32),
                        pltpu.VMEM((V7X_SUBLANES, V7X_LANES), F32)])
    est = 2 * 2 * g * PAGE_SIZE * ATT_WIDTH * 4 + 4 * PAGE_SIZE * ATT_WIDTH * 4
    return pl.pallas_call(
        functools.partial(_attn_sample_kernel, n_pages=g, n_q=n_q),
        grid_spec=grid_spec,
        out_shape=jax.ShapeDtypeStruct((nb, n_q, ATT_WIDTH), F32),
        compiler_params=pltpu.CompilerParams(
            dimension_semantics=("arbitrary", "arbitrary"),
            vmem_limit_bytes=_vmem_limit(est)),
        name="attn_sample",
    )(page_table, qbd, brow, umat, hmask, k_new_pad, v_new_pad, *([cache_k] * g), *([cache_v] * g))


def _merge_kernel(x_ref, o_ref, yc_ref, sga_ref, sgb_ref, gt1_ref, sh2_ref, sc2_ref, gt2_ref, g2_ref,
                  wa_ref, wc_ref, wo_ref, w1_ref, w2_ref, out_ref):
    ya = jnp.dot(o_ref[0], wa_ref[...], preferred_element_type=F32)
    yb = jnp.dot(yc_ref[0], wc_ref[...], preferred_element_type=F32)
    mixed = (sga_ref[0].astype(F32) * ya + sgb_ref[0].astype(F32) * yb).astype(BF16)
    x1 = x_ref[0] + gt1_ref[0] * jnp.dot(mixed, wo_ref[...], preferred_element_type=F32)
    ms = jnp.mean(x1 * x1, axis=-1, keepdims=True)
    h2 = x1 * lax.rsqrt(ms + EPS) * g2_ref[...]
    h2 = (h2 * (1.0 + sc2_ref[0]) + sh2_ref[0]).astype(BF16)
    chunk = D_MODEL
    mlp = jnp.zeros(x1.shape, F32)
    for c in range(D_FF // chunk):
        hid = jnp.maximum(jnp.dot(h2, w1_ref[:, c * chunk:(c + 1) * chunk], preferred_element_type=F32), 0.0)
        hid = (hid * hid).astype(BF16)
        mlp = mlp + jnp.dot(hid, w2_ref[c * chunk:(c + 1) * chunk, :], preferred_element_type=F32)
    out_ref[0] = x1 + gt2_ref[0] * mlp


def _merge(x, o_att, yc, sga, sgb, gt1, sh2, sc2, gt2, g2, wa, wc, wo, w1, w2, tm):
    nb, t, d = x.shape
    nt = t // tm
    mod_rows = gt1.shape[1]
    mod_block = (1, tm, d) if mod_rows == t else (1, 1, d)
    mod_map = (lambda b, i: (b, i, 0)) if mod_rows == t else (lambda b, i: (b, 0, 0))
    row = lambda w: pl.BlockSpec((1, tm, w), lambda b, i: (b, i, 0))
    mod = pl.BlockSpec(mod_block, mod_map)
    const = lambda shape: _resident(shape, lambda b, i: (0,) * len(shape))
    w_bytes = 2 * (2 * ATT_WIDTH * d + d * d + 2 * d * D_FF)
    est = w_bytes + 2 * tm * (4 * d + 2 * ATT_WIDTH * 2 + 2 * d * 2 + 4 * d) + 8 * tm * d * 4
    return pl.pallas_call(
        _merge_kernel,
        grid=(nb, nt),
        in_specs=[row(d), row(ATT_WIDTH), row(CONV_CH), row(d), row(d), mod, mod, mod, mod, const((1, d)),
                  const((ATT_WIDTH, d)), const((CONV_CH, d)), const((d, d)), const((d, D_FF)), const((D_FF, d))],
        out_specs=row(d),
        out_shape=jax.ShapeDtypeStruct((nb, t, d), F32),
        compiler_params=pltpu.CompilerParams(
            dimension_semantics=("arbitrary", "arbitrary"),
            vmem_limit_bytes=_vmem_limit(est)),
        name="merge_mlp",
    )(x, o_att, yc, sga, sgb, gt1, sh2, sc2, gt2, g2, wa, wc, wo, w1, w2)


def _neg_suffix_matrix(n, lhs_side):
    r = jnp.arange(n)[:, None]
    c = jnp.arange(n)[None, :]
    m = (c >= r) if lhs_side else (r >= c)
    return jnp.where(m, -1.0, 0.0).astype(BF16)


def kernel(x_prompt, x_sample, c_prompt, c_sample, cache_k, cache_v, state_conv, page_table, rms_g1, rms_g2,
           w_ada, b_ada, w_in, q_norm_g, k_norm_g, sb_bias, conv_w, conv_b, w_att_out, w_conv_out, w_o, w_mlp1,
           w_mlp2):
    depth = w_in.shape[0]
    assert depth == 1, "single-layer step"
    l = 0
    bp, seq, d = x_prompt.shape
    bs, dec, _ = x_sample.shape
    n_phys = cache_k.shape[1]

    w_in_bf = w_in[l].astype(BF16)
    wa, wc, wo = w_att_out[l].astype(BF16), w_conv_out[l].astype(BF16), w_o[l].astype(BF16)
    w1, w2 = w_mlp1[l].astype(BF16), w_mlp2[l].astype(BF16)
    g1 = rms_g1[l].reshape(1, d)
    g2 = rms_g2[l].reshape(1, d)
    qg = jnp.tile(q_norm_g[l], N_HEADS).reshape(1, ATT_WIDTH)
    kg = jnp.tile(k_norm_g[l], N_HEADS).reshape(1, ATT_WIDTH)
    cw = conv_w[l]
    cb = conv_b[l].reshape(1, CONV_CH)
    grp = jnp.arange(V7X_MXU_DIM) // HEAD_DIM
    gmat = jnp.where(grp[:, None] == grp[None, :], 1.0 / HEAD_DIM, 0.0).astype(BF16)
    bias = sb_bias[l].astype(F32)

    mod = _ada(jnp.concatenate([c_prompt, c_sample], axis=0), w_ada[l], b_ada[l])
    mods = [mod[:, j * d:(j + 1) * d] for j in range(N_MOD)]
    mp = [m[:bp].reshape(bp, 1, d) for m in mods]
    ms_ = [jnp.repeat(m[bp:], dec, axis=0).reshape(1, bs * dec, d) for m in mods]

    (q_p, k_p, kb_p, v_p, vb_p, yc_p, sga_p, sgb_p, cs_p) = _proj(
        x_prompt, mp[0], mp[1], g1, w_in_bf, qg, kg, gmat, cw, cb)
    o_p = _attn_prompt(q_p, kb_p, vb_p, bias, _neg_suffix_matrix(K_TILE, lhs_side=False))
    y_p = _merge(x_prompt, o_p, yc_p, sga_p, sgb_p, mp[2], mp[3], mp[4], mp[5], g2, wa, wc, wo, w1, w2,
                 tm=ROW_TILE)

    rows = bs * dec
    xs = x_sample.reshape(1, rows, d)
    st = state_conv[l]
    step = jnp.tile(jnp.arange(dec), bs)[:, None]
    flag1 = jnp.broadcast_to(step < 1, (rows, CONV_CH)).astype(F32)
    flag2 = jnp.broadcast_to(step < 2, (rows, CONV_CH)).astype(F32)
    zero = jnp.zeros((bs, dec - 2, CONV_CH), F32)
    val1 = jnp.concatenate([st[:, 1:2], zero, zero[:, :1]], axis=1).reshape(rows, CONV_CH)
    val2 = jnp.concatenate([st, zero], axis=1).reshape(rows, CONV_CH)
    hist = (jnp.stack([flag1, val1]), jnp.stack([flag2, val2]))
    (q_s, k_s, _, v_s, _, yc_s, sga_s, sgb_s, u_s) = _proj(
        xs, ms_[0], ms_[1], g1, w_in_bf, qg, kg, gmat, cw, cb, hist=hist)

    q4 = q_s.reshape(bs, dec, N_HEADS, HEAD_DIM)
    eye = jnp.eye(N_HEADS, dtype=BF16)
    qbd = jnp.einsum('bthd,hg->bhdtg', q4, eye).reshape(bs, ATT_WIDTH, dec * N_HEADS)
    qbd = jnp.pad(qbd, ((0, 0), (0, 0), (0, V7X_LANES - dec * N_HEADS)))
    brow = jnp.pad(jnp.tile(bias, dec), (0, V7X_LANES - dec * N_HEADS)).reshape(1, V7X_LANES)
    hmask = (jnp.arange(ATT_WIDTH)[None, :] // HEAD_DIM == jnp.arange(N_HEADS)[:, None]).astype(F32)
    pad_rows = ((0, 0), (0, PAGE_SIZE - dec), (0, 0))
    k_new_pad = jnp.pad(k_s.reshape(bs, dec, ATT_WIDTH), pad_rows)
    v_new_pad = jnp.pad(v_s.reshape(bs, dec, ATT_WIDTH), pad_rows)
    o_s = _attn_sample(page_table, qbd, brow, _neg_suffix_matrix(PAGE_SIZE, lhs_side=True), hmask,
                       k_new_pad, v_new_pad,
                       cache_k[l].reshape(n_phys, PAGE_SIZE, ATT_WIDTH),
                       cache_v[l].reshape(n_phys, PAGE_SIZE, ATT_WIDTH), n_q=dec)
    o_s = o_s.reshape(1, rows, ATT_WIDTH).astype(BF16)
    y_s = _merge(xs, o_s, yc_s, sga_s, sgb_s, ms_[2], ms_[3], ms_[4], ms_[5], g2, wa, wc, wo, w1, w2, tm=rows)

    heads = (N_HEADS, HEAD_DIM)
    return (y_p,
            y_s.reshape(bs, dec, d),
            k_p.reshape(1, bp, seq, *heads), v_p.reshape(1, bp, seq, *heads),
            cs_p.reshape(1, bp, CONV_WIDTH - 1, CONV_CH),
            k_s.reshape(1, bs, dec, *heads), v_s.reshape(1, bs, dec, *heads),
            u_s.reshape(bs, dec, CONV_CH)[:, dec - (CONV_WIDTH - 1):].reshape(1, bs, CONV_WIDTH - 1, CONV_CH))
```

```python
import functools
import math

import jax
import jax.numpy as jnp
from jax import lax
from jax.experimental import pallas as pl
from jax.experimental.pallas import tpu as pltpu

D_MODEL = 1024
N_HEADS = 8
HEAD_DIM = 64
ATT_WIDTH = N_HEADS * HEAD_DIM
CONV_CH = 512
CONV_WIDTH = 3
D_FF = 4 * D_MODEL
N_MOD = 6
PAGE_SIZE = 128
EPS = 1e-6
SB_SCALE = 1.0 / math.sqrt(HEAD_DIM)
IN_WIDTH = 3 * ATT_WIDTH + 3 * CONV_CH + 2 * D_MODEL

V7X_LANES = 128
V7X_SUBLANES = 8
V7X_MXU_DIM = 256
V7X_VMEM_BYTES = 64 * 1024 * 1024

ROW_TILE = 512
K_TILE = V7X_MXU_DIM
Q_TILE = K_TILE
PAGES_PER_STEP = 16
HIST_ROWS = V7X_SUBLANES
SOFTPLUS_LINEAR_FROM = 40.0

F32 = jnp.float32
BF16 = jnp.bfloat16


def _vmem_limit(nbytes):
    return int(min(nbytes + (8 << 20), V7X_VMEM_BYTES - (6 << 20)))


def _resident(shape, index_map):
    return pl.BlockSpec(shape, index_map, pipeline_mode=pl.Buffered(1))


def _mm(lhs, rhs):
    return lax.dot_general(lhs, rhs, (((1,), (0,)), ((), ())), preferred_element_type=F32)


def _softplus(z):
    return jnp.maximum(jnp.log(1.0 + jnp.exp(jnp.minimum(z, SOFTPLUS_LINEAR_FROM))), z)


def _ada_kernel(c_ref, w_ref, b_ref, o_ref):
    c = c_ref[...]
    s = (c * jax.nn.sigmoid(c)).astype(BF16)
    o_ref[...] = jnp.dot(s, w_ref[...], preferred_element_type=F32) + b_ref[...]


def _ada(c_all, w_ada, b_ada):
    n, d = c_all.shape
    width = w_ada.shape[1]
    tn = 1536
    return pl.pallas_call(
        _ada_kernel,
        grid=(width // tn,),
        in_specs=[pl.BlockSpec((n, d), lambda j: (0, 0)),
                  pl.BlockSpec((d, tn), lambda j: (0, j)),
                  pl.BlockSpec((1, tn), lambda j: (0, j))],
        out_specs=pl.BlockSpec((n, tn), lambda j: (0, j)),
        out_shape=jax.ShapeDtypeStruct((n, width), F32),
        compiler_params=pltpu.CompilerParams(
            dimension_semantics=("arbitrary",),
            vmem_limit_bytes=_vmem_limit(2 * d * tn * 4)),
        name="ada",
    )(c_all, w_ada, b_ada.reshape(1, width))


def _head_rms(p, gmat_ref, gain):
    sq = (p * p).astype(BF16)
    parts = []
    for c in range(ATT_WIDTH // V7X_MXU_DIM):
        sl = slice(c * V7X_MXU_DIM, (c + 1) * V7X_MXU_DIM)
        parts.append(jnp.dot(sq[:, sl], gmat_ref[...], preferred_element_type=F32))
    ms = jnp.concatenate(parts, axis=-1)
    return p * lax.rsqrt(ms + EPS) * gain


def _proj_kernel(*refs, tm, sample_mode):
    if sample_mode:
        (x_ref, sh_ref, sc_ref, g1_ref, w_ref, qg_ref, kg_ref, gmat_ref, cw_ref, cb_ref, h1_ref, h2_ref,
         q_out, k_out, kb_out, v_out, vb_out, yc_out, sga_out, sgb_out, u_out, conv_buf) = refs
    else:
        (x_ref, sh_ref, sc_ref, g1_ref, w_ref, qg_ref, kg_ref, gmat_ref, cw_ref, cb_ref,
         q_out, k_out, kb_out, v_out, vb_out, yc_out, sga_out, sgb_out, cs_out, conv_buf) = refs

    x = x_ref[0]
    ms = jnp.mean(x * x, axis=-1, keepdims=True)
    h = x * lax.rsqrt(ms + EPS) * g1_ref[...]
    h = (h * (1.0 + sc_ref[0]) + sh_ref[0]).astype(BF16)

    def part(lo, width):
        return jnp.dot(h, w_ref[:, lo:lo + width], preferred_element_type=F32)

    a = ATT_WIDTH
    qn = _head_rms(part(0, a), gmat_ref, qg_ref[...])
    q_out[0] = (qn * SB_SCALE).astype(BF16)
    kn = _head_rms(part(a, a), gmat_ref, kg_ref[...])
    k_out[0] = kn
    kb_out[0] = kn.astype(BF16)
    v = part(2 * a, a)
    v_out[0] = v
    vb_out[0] = v.astype(BF16)

    c0 = 3 * a
    hc = part(c0, CONV_CH)
    bgate = part(c0 + CONV_CH, CONV_CH)
    cgate = part(c0 + 2 * CONV_CH, CONV_CH)
    u = cgate * hc

    conv_buf[HIST_ROWS:HIST_ROWS + tm, :] = u
    if sample_mode:
        u_out[0] = u
        conv_buf[0:HIST_ROWS, :] = jnp.zeros((HIST_ROWS, CONV_CH), F32)
    else:
        @pl.when(pl.program_id(1) == 0)
        def _():
            conv_buf[0:HIST_ROWS, :] = jnp.zeros((HIST_ROWS, CONV_CH), F32)

    u1 = conv_buf[HIST_ROWS - 1:HIST_ROWS - 1 + tm, :]
    u2 = conv_buf[HIST_ROWS - 2:HIST_ROWS - 2 + tm, :]
    if sample_mode:
        u1 = jnp.where(h1_ref[0] != 0, h1_ref[1], u1)
        u2 = jnp.where(h2_ref[0] != 0, h2_ref[1], u2)
    y = cb_ref[...] + cw_ref[0:1, :] * u2 + cw_ref[1:2, :] * u1 + cw_ref[2:3, :] * u
    yc_out[0] = (bgate * y).astype(BF16)
    if not sample_mode:
        conv_buf[0:HIST_ROWS, :] = u[tm - HIST_ROWS:, :]

        @pl.when(pl.program_id(1) == pl.num_programs(1) - 1)
        def _():
            cs_out[0] = u[tm - (CONV_WIDTH - 1):, :]

    g0 = c0 + 3 * CONV_CH
    sga_out[0] = jax.nn.sigmoid(part(g0, D_MODEL)).astype(BF16)
    sgb_out[0] = jax.nn.sigmoid(part(g0 + D_MODEL, D_MODEL)).astype(BF16)


def _proj(x, sh, sc, g1, w_in_bf, qg, kg, gmat, conv_w, conv_b, hist=None):
    nb, t, d = x.shape
    sample_mode = hist is not None
    tm = t if sample_mode else ROW_TILE
    nt = t // tm
    mod_rows = sh.shape[1]
    mod_block = (1, tm, d) if mod_rows == t else (1, 1, d)
    mod_map = (lambda b, i: (b, i, 0)) if mod_rows == t else (lambda b, i: (b, 0, 0))
    row = lambda w: pl.BlockSpec((1, tm, w), lambda b, i: (b, i, 0))
    const = lambda shape: _resident(shape, lambda b, i: (0,) * len(shape))

    in_specs = [row(d), pl.BlockSpec(mod_block, mod_map), pl.BlockSpec(mod_block, mod_map),
                const((1, d)), const((d, IN_WIDTH)), const((1, ATT_WIDTH)), const((1, ATT_WIDTH)),
                const((V7X_MXU_DIM, V7X_MXU_DIM)), const((CONV_WIDTH, CONV_CH)), const((1, CONV_CH))]
    args = [x, sh, sc, g1, w_in_bf, qg, kg, gmat, conv_w, conv_b]
    out_shape = [jax.ShapeDtypeStruct((nb, t, ATT_WIDTH), BF16),
                 jax.ShapeDtypeStruct((nb, t, ATT_WIDTH), F32),
                 jax.ShapeDtypeStruct((nb, t, ATT_WIDTH), BF16),
                 jax.ShapeDtypeStruct((nb, t, ATT_WIDTH), F32),
                 jax.ShapeDtypeStruct((nb, t, ATT_WIDTH), BF16),
                 jax.ShapeDtypeStruct((nb, t, CONV_CH), BF16),
                 jax.ShapeDtypeStruct((nb, t, D_MODEL), BF16),
                 jax.ShapeDtypeStruct((nb, t, D_MODEL), BF16)]
    out_specs = [row(ATT_WIDTH)] * 5 + [row(CONV_CH), row(D_MODEL), row(D_MODEL)]
    if sample_mode:
        in_specs += [const((2, t, CONV_CH)), const((2, t, CONV_CH))]
        args += list(hist)
        out_shape.append(jax.ShapeDtypeStruct((nb, t, CONV_CH), F32))
        out_specs.append(row(CONV_CH))
    else:
        out_shape.append(jax.ShapeDtypeStruct((nb, CONV_WIDTH - 1, CONV_CH), F32))
        out_specs.append(pl.BlockSpec((1, CONV_WIDTH - 1, CONV_CH), lambda b, i: (b, 0, 0)))

    est = (d * IN_WIDTH * 2 + 2 * tm * d * 4 + 2 * tm * (ATT_WIDTH * 14 + CONV_CH * 6 + D_MODEL * 4)
           + 6 * tm * D_MODEL * 4)
    return pl.pallas_call(
        functools.partial(_proj_kernel, tm=tm, sample_mode=sample_mode),
        grid=(nb, nt),
        in_specs=in_specs,
        out_specs=out_specs,
        out_shape=out_shape,
        scratch_shapes=[pltpu.VMEM((tm + HIST_ROWS, CONV_CH), F32)],
        compiler_params=pltpu.CompilerParams(
            dimension_semantics=("arbitrary", "arbitrary"),
            vmem_limit_bytes=_vmem_limit(est)),
        name="proj_sample" if sample_mode else "proj_prompt",
    )(*args)


def _attn_prompt_kernel(bias_ref, q_ref, k_ref, v_ref, u_ref, o_ref):
    i = pl.program_id(1)
    tq = q_ref.shape[1]
    pair = V7X_LANES
    n_pairs = q_ref.shape[2] // pair
    per_pair = pair // HEAD_DIM
    lane_q = lax.broadcasted_iota(jnp.int32, (tq, pair), 1)
    lane_k = lax.broadcasted_iota(jnp.int32, (K_TILE, pair), 1)
    qms, biases = [], []
    for hp in range(n_pairs):
        q2 = q_ref[0, :, hp * pair:(hp + 1) * pair]
        for hh in range(per_pair):
            qms.append(jnp.where(lane_q // HEAD_DIM == hh, q2, jnp.zeros_like(q2)))
            biases.append(bias_ref[hp * per_pair + hh])

    def tile(j, carries, accs, diagonal):
        k0 = pl.multiple_of(j * K_TILE, K_TILE)
        if diagonal:
            row = lax.broadcasted_iota(jnp.int32, (tq, K_TILE), 0)
            col = lax.broadcasted_iota(jnp.int32, (tq, K_TILE), 1)
            valid = row > col
        n_heads = n_pairs * per_pair
        zs = []
        for h in range(n_heads):
            hp = h // per_pair
            kt = k_ref[0, pl.ds(k0, K_TILE), hp * pair:(hp + 1) * pair]
            zs.append(lax.dot_general(qms[h], kt, (((1,), (1,)), ((), ())), preferred_element_type=F32)
                      + biases[h])
        ps = []
        for h in range(n_heads):
            sp = _softplus(zs[h])
            if diagonal:
                sp = jnp.where(valid, sp, 0.0)
            ps.append(_mm(sp, u_ref[...]))
        new_carries, outs = [], []
        for h in range(n_heads):
            hp, hh = divmod(h, per_pair)
            a = jnp.exp(zs[h] + ps[h] + carries[h])
            if diagonal:
                a = jnp.where(valid, a, 0.0)
            new_carries.append(carries[h] + ps[h][:, 0:1])
            vt = v_ref[0, pl.ds(k0, K_TILE), hp * pair:(hp + 1) * pair]
            vh = jnp.where(lane_k // HEAD_DIM == hh, vt, jnp.zeros_like(vt))
            outs.append(_mm(a, vh))
        new_accs = [accs[hp] + sum(outs[hp * per_pair:(hp + 1) * per_pair][1:], outs[hp * per_pair])
                    for hp in range(n_pairs)]
        return tuple(new_carries), tuple(new_accs)

    carries = tuple(jnp.zeros((tq, 1), F32) for _ in range(n_pairs * per_pair))
    accs = tuple(jnp.zeros((tq, pair), F32) for _ in range(n_pairs))
    carries, accs = tile(i, carries, accs, diagonal=True)

    def body(n, c):
        return tile(i - 1 - n, c[0], c[1], diagonal=False)

    carries, accs = lax.fori_loop(0, i, body, (carries, accs))
    for hp in range(n_pairs):
        o_ref[0, :, hp * pair:(hp + 1) * pair] = accs[hp].astype(o_ref.dtype)


def _attn_prompt(q_bf, k_bf, v_bf, bias, umat):
    nb, t, width = q_bf.shape
    tq = Q_TILE
    est = 2 * t * width * 2 + 4 * tq * width * 2 + 48 * tq * K_TILE * 4
    return pl.pallas_call(
        _attn_prompt_kernel,
        grid=(nb, t // tq),
        in_specs=[pl.BlockSpec(memory_space=pltpu.SMEM),
                  pl.BlockSpec((1, tq, width), lambda b, i: (b, i, 0)),
                  _resident((1, t, width), lambda b, i: (b, 0, 0)),
                  _resident((1, t, width), lambda b, i: (b, 0, 0)),
                  _resident((K_TILE, K_TILE), lambda b, i: (0, 0))],
        out_specs=pl.BlockSpec((1, tq, width), lambda b, i: (b, i, 0)),
        out_shape=jax.ShapeDtypeStruct((nb, t, width), BF16),
        compiler_params=pltpu.CompilerParams(
            dimension_semantics=("arbitrary", "arbitrary"),
            vmem_limit_bytes=_vmem_limit(est)),
        name="attn_prompt",
    )(bias, q_bf, k_bf, v_bf, umat)


SAMPLE_ROWS = 32
QUAD = V7X_LANES // SAMPLE_ROWS


def _attn_sample_kernel(pt_ref, qbd_ref, bcol_ref, u_ref, knt_ref, vnt_ref, *rest, n_pages):
    del pt_ref
    kt_refs = rest[:n_pages]
    vt_refs = rest[n_pages:2 * n_pages]
    o_ref, acc_ref, carry_ref = rest[2 * n_pages:]
    c = pl.program_id(1)
    rows = SAMPLE_ROWS
    qbd = qbd_ref[0]
    bcol = bcol_ref[...]
    col_block = lax.broadcasted_iota(jnp.int32, (PAGE_SIZE, V7X_LANES), 1) // rows

    def scores(kt):
        return jnp.dot(qbd, kt, preferred_element_type=F32) + bcol

    def suffix(sp):
        return jnp.dot(sp.astype(BF16), u_ref[...], preferred_element_type=F32)

    def transposed_weights(a_blocks):
        at = jnp.concatenate(a_blocks, axis=0).T
        return [jnp.where(col_block == p, at, 0.0) for p in range(len(a_blocks))]

    @pl.when(c == 0)
    def _():
        z = scores(knt_ref[0])
        key = lax.broadcasted_iota(jnp.int32, (rows, PAGE_SIZE), 1)
        qry = lax.broadcasted_iota(jnp.int32, (rows, PAGE_SIZE), 0) // N_HEADS
        valid = key < qry
        sp = jnp.where(valid, _softplus(z), 0.0)
        p = suffix(sp)
        a = jnp.where(valid, jnp.exp(z + p), 0.0)
        zero = jnp.zeros_like(a)
        w = transposed_weights([a] + [zero] * (QUAD - 1))[0]
        acc_ref[...] = jnp.dot(vnt_ref[0], w, preferred_element_type=F32)
        carry_ref[...] = p[:, 0:1]

    zs = [scores(kt_refs[g][0]) for g in range(n_pages)]
    z_all = jnp.concatenate(zs, axis=0)
    p_all = suffix(_softplus(z_all))
    carry = carry_ref[...]
    carries = [None] * n_pages
    for g in reversed(range(n_pages)):
        carries[g] = carry
        carry = carry + p_all[g * rows:(g + 1) * rows, 0:1]
    carry_ref[...] = carry
    a_all = jnp.exp(z_all + p_all + jnp.concatenate(carries, axis=0))

    acc = acc_ref[...]
    for q0 in range(0, n_pages, QUAD):
        w = transposed_weights([a_all[(q0 + p) * rows:(q0 + p + 1) * rows, :] for p in range(QUAD)])
        for p in range(0, QUAD, 2):
            vt2 = jnp.concatenate([vt_refs[q0 + p][0], vt_refs[q0 + p + 1][0]], axis=1)
            w2 = jnp.concatenate([w[p], w[p + 1]], axis=0)
            acc = acc + jnp.dot(vt2, w2, preferred_element_type=F32)
    acc_ref[...] = acc

    @pl.when(c == pl.num_programs(1) - 1)
    def _():
        total = acc
        for p in range(1, QUAD):
            total = total + pltpu.roll(acc, p * rows, axis=1)
        o_ref[0] = total


def _attn_sample(page_table, qbd, bcol, umat, knt, vnt, cache_kt, cache_vt):
    nb, n_tab = page_table.shape
    g = PAGES_PER_STEP
    n_chunks = n_tab // g
    rows = SAMPLE_ROWS
    assert qbd.shape[1] == rows and g % QUAD == 0
    page_block = (1, ATT_WIDTH, PAGE_SIZE)

    def page_spec(slot):
        return pl.BlockSpec(page_block, lambda b, c, pt: (pt[b, (n_chunks - 1 - c) * g + slot], 0, 0))

    per_seq = lambda shape: pl.BlockSpec(shape, lambda b, c, pt: (b, 0, 0))
    const = lambda shape: pl.BlockSpec(shape, lambda b, c, pt: (0,) * len(shape))
    in_specs = ([per_seq((1, rows, ATT_WIDTH)), const((rows, 1)), const((PAGE_SIZE, PAGE_SIZE)),
                 per_seq(page_block), per_seq(page_block)]
                + [page_spec(s) for s in range(g)] + [page_spec(s) for s in range(g)])
    grid_spec = pltpu.PrefetchScalarGridSpec(
        num_scalar_prefetch=1,
        grid=(nb, n_chunks),
        in_specs=in_specs,
        out_specs=pl.BlockSpec((1, ATT_WIDTH, V7X_LANES), lambda b, c, pt: (b, 0, 0)),
        scratch_shapes=[pltpu.VMEM((ATT_WIDTH, V7X_LANES), F32),
                        pltpu.VMEM((rows, 1), F32)])
    est = 2 * 2 * g * PAGE_SIZE * ATT_WIDTH * 4 + 8 * PAGE_SIZE * ATT_WIDTH * 4
    return pl.pallas_call(
        functools.partial(_attn_sample_kernel, n_pages=g),
        grid_spec=grid_spec,
        out_shape=jax.ShapeDtypeStruct((nb, ATT_WIDTH, V7X_LANES), F32),
        compiler_params=pltpu.CompilerParams(
            dimension_semantics=("arbitrary", "arbitrary"),
            vmem_limit_bytes=_vmem_limit(est)),
        name="attn_sample",
    )(page_table, qbd, bcol, umat, knt, vnt, *([cache_kt] * g), *([cache_vt] * g))


def _merge_kernel(x_ref, o_ref, yc_ref, sga_ref, sgb_ref, gt1_ref, sh2_ref, sc2_ref, gt2_ref, g2_ref,
                  wa_ref, wc_ref, wo_ref, w1_ref, w2_ref, out_ref):
    ya = jnp.dot(o_ref[0], wa_ref[...], preferred_element_type=F32)
    yb = jnp.dot(yc_ref[0], wc_ref[...], preferred_element_type=F32)
    mixed = (sga_ref[0].astype(F32) * ya + sgb_ref[0].astype(F32) * yb).astype(BF16)
    x1 = x_ref[0] + gt1_ref[0] * jnp.dot(mixed, wo_ref[...], preferred_element_type=F32)
    ms = jnp.mean(x1 * x1, axis=-1, keepdims=True)
    h2 = x1 * lax.rsqrt(ms + EPS) * g2_ref[...]
    h2 = (h2 * (1.0 + sc2_ref[0]) + sh2_ref[0]).astype(BF16)
    chunk = D_MODEL
    mlp = jnp.zeros(x1.shape, F32)
    for c in range(D_FF // chunk):
        hid = jnp.maximum(jnp.dot(h2, w1_ref[:, c * chunk:(c + 1) * chunk], preferred_element_type=F32), 0.0)
        hid = (hid * hid).astype(BF16)
        mlp = mlp + jnp.dot(hid, w2_ref[c * chunk:(c + 1) * chunk, :], preferred_element_type=F32)
    out_ref[0] = x1 + gt2_ref[0] * mlp


def _merge(x, o_att, yc, sga, sgb, gt1, sh2, sc2, gt2, g2, wa, wc, wo, w1, w2, tm):
    nb, t, d = x.shape
    nt = t // tm
    mod_rows = gt1.shape[1]
    mod_block = (1, tm, d) if mod_rows == t else (1, 1, d)
    mod_map = (lambda b, i: (b, i, 0)) if mod_rows == t else (lambda b, i: (b, 0, 0))
    row = lambda w: pl.BlockSpec((1, tm, w), lambda b, i: (b, i, 0))
    mod = pl.BlockSpec(mod_block, mod_map)
    const = lambda shape: _resident(shape, lambda b, i: (0,) * len(shape))
    w_bytes = 2 * (2 * ATT_WIDTH * d + d * d + 2 * d * D_FF)
    est = w_bytes + 2 * tm * (4 * d + 2 * ATT_WIDTH * 2 + 2 * d * 2 + 4 * d) + 8 * tm * d * 4
    return pl.pallas_call(
        _merge_kernel,
        grid=(nb, nt),
        in_specs=[row(d), row(ATT_WIDTH), row(CONV_CH), row(d), row(d), mod, mod, mod, mod, const((1, d)),
                  const((ATT_WIDTH, d)), const((CONV_CH, d)), const((d, d)), const((d, D_FF)), const((D_FF, d))],
        out_specs=row(d),
        out_shape=jax.ShapeDtypeStruct((nb, t, d), F32),
        compiler_params=pltpu.CompilerParams(
            dimension_semantics=("arbitrary", "arbitrary"),
            vmem_limit_bytes=_vmem_limit(est)),
        name="merge_mlp",
    )(x, o_att, yc, sga, sgb, gt1, sh2, sc2, gt2, g2, wa, wc, wo, w1, w2)


def _neg_suffix_matrix(n):
    j = jnp.arange(n)[:, None]
    s = jnp.arange(n)[None, :]
    return jnp.where(j >= s, -1.0, 0.0).astype(BF16)


def _slot_minor_pages(cache):
    n_phys, slots, heads, dim = cache.shape
    return jnp.transpose(cache, (0, 2, 3, 1)).reshape(n_phys, heads * dim, slots)


def kernel(x_prompt, x_sample, c_prompt, c_sample, cache_k, cache_v, state_conv, page_table, rms_g1, rms_g2,
           w_ada, b_ada, w_in, q_norm_g, k_norm_g, sb_bias, conv_w, conv_b, w_att_out, w_conv_out, w_o, w_mlp1,
           w_mlp2):
    depth = w_in.shape[0]
    assert depth == 1, "single-layer step"
    l = 0
    bp, seq, d = x_prompt.shape
    bs, dec, _ = x_sample.shape
    assert dec * N_HEADS == SAMPLE_ROWS

    w_in_bf = w_in[l].astype(BF16)
    wa, wc, wo = w_att_out[l].astype(BF16), w_conv_out[l].astype(BF16), w_o[l].astype(BF16)
    w1, w2 = w_mlp1[l].astype(BF16), w_mlp2[l].astype(BF16)
    g1 = rms_g1[l].reshape(1, d)
    g2 = rms_g2[l].reshape(1, d)
    qg = jnp.tile(q_norm_g[l], N_HEADS).reshape(1, ATT_WIDTH)
    kg = jnp.tile(k_norm_g[l], N_HEADS).reshape(1, ATT_WIDTH)
    cw = conv_w[l]
    cb = conv_b[l].reshape(1, CONV_CH)
    grp = jnp.arange(V7X_MXU_DIM) // HEAD_DIM
    gmat = jnp.where(grp[:, None] == grp[None, :], 1.0 / HEAD_DIM, 0.0).astype(BF16)
    bias = sb_bias[l].astype(F32)

    mod = _ada(jnp.concatenate([c_prompt, c_sample], axis=0), w_ada[l], b_ada[l])
    mods = [mod[:, j * d:(j + 1) * d] for j in range(N_MOD)]
    mp = [m[:bp].reshape(bp, 1, d) for m in mods]
    ms_ = [jnp.repeat(m[bp:], dec, axis=0).reshape(1, bs * dec, d) for m in mods]

    (q_p, k_p, kb_p, v_p, vb_p, yc_p, sga_p, sgb_p, cs_p) = _proj(
        x_prompt, mp[0], mp[1], g1, w_in_bf, qg, kg, gmat, cw, cb)
    o_p = _attn_prompt(q_p, kb_p, vb_p, bias, _neg_suffix_matrix(K_TILE))
    y_p = _merge(x_prompt, o_p, yc_p, sga_p, sgb_p, mp[2], mp[3], mp[4], mp[5], g2, wa, wc, wo, w1, w2,
                 tm=ROW_TILE)

    rows = bs * dec
    xs = x_sample.reshape(1, rows, d)
    st = state_conv[l]
    step = jnp.tile(jnp.arange(dec), bs)[:, None]
    flag1 = jnp.broadcast_to(step < 1, (rows, CONV_CH)).astype(F32)
    flag2 = jnp.broadcast_to(step < 2, (rows, CONV_CH)).astype(F32)
    zero = jnp.zeros((bs, dec - 2, CONV_CH), F32)
    val1 = jnp.concatenate([st[:, 1:2], zero, zero[:, :1]], axis=1).reshape(rows, CONV_CH)
    val2 = jnp.concatenate([st, zero], axis=1).reshape(rows, CONV_CH)
    hist = (jnp.stack([flag1, val1]), jnp.stack([flag2, val2]))
    (q_s, k_s, _, v_s, _, yc_s, sga_s, sgb_s, u_s) = _proj(
        xs, ms_[0], ms_[1], g1, w_in_bf, qg, kg, gmat, cw, cb, hist=hist)

    q4 = q_s.reshape(bs, dec, N_HEADS, HEAD_DIM)
    eye = jnp.eye(N_HEADS, dtype=BF16)
    qbd = jnp.einsum('bthd,hg->bthgd', q4, eye).reshape(bs, SAMPLE_ROWS, ATT_WIDTH)
    bcol = jnp.tile(bias, dec).reshape(SAMPLE_ROWS, 1)
    pad_slots = ((0, 0), (0, 0), (0, PAGE_SIZE - dec))
    knt = jnp.pad(jnp.transpose(k_s.reshape(bs, dec, ATT_WIDTH), (0, 2, 1)), pad_slots)
    vnt = jnp.pad(jnp.transpose(v_s.reshape(bs, dec, ATT_WIDTH), (0, 2, 1)), pad_slots)
    ot = _attn_sample(page_table, qbd, bcol, _neg_suffix_matrix(PAGE_SIZE), knt, vnt,
                      _slot_minor_pages(cache_k[l]), _slot_minor_pages(cache_v[l]))
    ot = ot[:, :, :SAMPLE_ROWS].reshape(bs, N_HEADS, HEAD_DIM, dec, N_HEADS)
    o_s = jnp.einsum('bhdth->bthd', ot).reshape(1, rows, ATT_WIDTH).astype(BF16)
    y_s = _merge(xs, o_s, yc_s, sga_s, sgb_s, ms_[2], ms_[3], ms_[4], ms_[5], g2, wa, wc, wo, w1, w2, tm=rows)

    heads = (N_HEADS, HEAD_DIM)
    return (y_p,
            y_s.reshape(bs, dec, d),
            k_p.reshape(1, bp, seq, *heads), v_p.reshape(1, bp, seq, *heads),
            cs_p.reshape(1, bp, CONV_WIDTH - 1, CONV_CH),
            k_s.reshape(1, bs, dec, *heads), v_s.reshape(1, bs, dec, *heads),
            u_s.reshape(bs, dec, CONV_CH)[:, dec - (CONV_WIDTH - 1):].reshape(1, bs, CONV_WIDTH - 1, CONV_CH))
```

```python
import functools
import math

import jax
import jax.numpy as jnp
from jax import lax
from jax.experimental import pallas as pl
from jax.experimental.pallas import tpu as pltpu

D_MODEL = 1024
N_HEADS = 8
HEAD_DIM = 64
ATT_WIDTH = N_HEADS * HEAD_DIM
CONV_CH = 512
CONV_WIDTH = 3
D_FF = 4 * D_MODEL
N_MOD = 6
PAGE_SIZE = 128
EPS = 1e-6
SB_SCALE = 1.0 / math.sqrt(HEAD_DIM)
LOG2E = math.log2(math.e)
Q_SCALE = SB_SCALE * LOG2E
IN_WIDTH = 3 * ATT_WIDTH + 3 * CONV_CH + 2 * D_MODEL

V7X_LANES = 128
V7X_SUBLANES = 8
V7X_MXU_DIM = 256
V7X_VMEM_BYTES = 64 * 1024 * 1024

ROW_TILE = 512
K_TILE = V7X_MXU_DIM
Q_TILE = K_TILE
PAGES_PER_STEP = 16
HIST_ROWS = V7X_SUBLANES
SOFTPLUS2_LINEAR_FROM = 60.0
MASKED_LOG_WEIGHT = -1e30

F32 = jnp.float32
BF16 = jnp.bfloat16


def _vmem_limit(nbytes):
    return int(min(nbytes + (8 << 20), V7X_VMEM_BYTES - (6 << 20)))


def _resident(shape, index_map):
    return pl.BlockSpec(shape, index_map, pipeline_mode=pl.Buffered(1))


def _mm(lhs, rhs):
    return lax.dot_general(lhs, rhs, (((1,), (0,)), ((), ())), preferred_element_type=F32)


def _softplus2(z2):
    return jnp.log(1.0 + jnp.exp2(z2)) * LOG2E


def _ada_kernel(c_ref, w_ref, b_ref, o_ref):
    c = c_ref[...]
    s = (c * jax.nn.sigmoid(c)).astype(BF16)
    o_ref[...] = jnp.dot(s, w_ref[...], preferred_element_type=F32) + b_ref[...]


def _ada(c_all, w_ada, b_ada):
    n, d = c_all.shape
    width = w_ada.shape[1]
    tn = 1536
    return pl.pallas_call(
        _ada_kernel,
        grid=(width // tn,),
        in_specs=[pl.BlockSpec((n, d), lambda j: (0, 0)),
                  pl.BlockSpec((d, tn), lambda j: (0, j)),
                  pl.BlockSpec((1, tn), lambda j: (0, j))],
        out_specs=pl.BlockSpec((n, tn), lambda j: (0, j)),
        out_shape=jax.ShapeDtypeStruct((n, width), F32),
        compiler_params=pltpu.CompilerParams(
            dimension_semantics=("arbitrary",),
            vmem_limit_bytes=_vmem_limit(2 * d * tn * 4)),
        name="ada",
    )(c_all, w_ada, b_ada.reshape(1, width))


def _head_rms(p, gmat_ref, gain):
    sq = (p * p).astype(BF16)
    parts = []
    for c in range(ATT_WIDTH // V7X_MXU_DIM):
        sl = slice(c * V7X_MXU_DIM, (c + 1) * V7X_MXU_DIM)
        parts.append(jnp.dot(sq[:, sl], gmat_ref[...], preferred_element_type=F32))
    ms = jnp.concatenate(parts, axis=-1)
    return p * lax.rsqrt(ms + EPS) * gain


def _proj_kernel(*refs, tm, sample_mode):
    if sample_mode:
        (x_ref, sh_ref, sc_ref, g1_ref, w_ref, qg_ref, kg_ref, gmat_ref, cw_ref, cb_ref, h1_ref, h2_ref,
         q_out, k_out, kb_out, v_out, vb_out, yc_out, sga_out, sgb_out, u_out, conv_buf) = refs
    else:
        (x_ref, sh_ref, sc_ref, g1_ref, w_ref, qg_ref, kg_ref, gmat_ref, cw_ref, cb_ref,
         q_out, k_out, kb_out, v_out, vb_out, yc_out, sga_out, sgb_out, cs_out, conv_buf) = refs

    x = x_ref[0]
    ms = jnp.mean(x * x, axis=-1, keepdims=True)
    h = x * lax.rsqrt(ms + EPS) * g1_ref[...]
    h = (h * (1.0 + sc_ref[0]) + sh_ref[0]).astype(BF16)

    def part(lo, width):
        return jnp.dot(h, w_ref[:, lo:lo + width], preferred_element_type=F32)

    a = ATT_WIDTH
    qn = _head_rms(part(0, a), gmat_ref, qg_ref[...])
    q_out[0] = (qn * Q_SCALE).astype(BF16)
    kn = _head_rms(part(a, a), gmat_ref, kg_ref[...])
    k_out[0] = kn
    kb_out[0] = kn.astype(BF16)
    v = part(2 * a, a)
    v_out[0] = v
    vb_out[0] = v.astype(BF16)

    c0 = 3 * a
    hc = part(c0, CONV_CH)
    bgate = part(c0 + CONV_CH, CONV_CH)
    cgate = part(c0 + 2 * CONV_CH, CONV_CH)
    u = cgate * hc

    conv_buf[HIST_ROWS:HIST_ROWS + tm, :] = u
    if sample_mode:
        u_out[0] = u
        conv_buf[0:HIST_ROWS, :] = jnp.zeros((HIST_ROWS, CONV_CH), F32)
    else:
        @pl.when(pl.program_id(1) == 0)
        def _():
            conv_buf[0:HIST_ROWS, :] = jnp.zeros((HIST_ROWS, CONV_CH), F32)

    u1 = conv_buf[HIST_ROWS - 1:HIST_ROWS - 1 + tm, :]
    u2 = conv_buf[HIST_ROWS - 2:HIST_ROWS - 2 + tm, :]
    if sample_mode:
        u1 = jnp.where(h1_ref[0] != 0, h1_ref[1], u1)
        u2 = jnp.where(h2_ref[0] != 0, h2_ref[1], u2)
    y = cb_ref[...] + cw_ref[0:1, :] * u2 + cw_ref[1:2, :] * u1 + cw_ref[2:3, :] * u
    yc_out[0] = (bgate * y).astype(BF16)
    if not sample_mode:
        conv_buf[0:HIST_ROWS, :] = u[tm - HIST_ROWS:, :]

        @pl.when(pl.program_id(1) == pl.num_programs(1) - 1)
        def _():
            cs_out[0] = u[tm - (CONV_WIDTH - 1):, :]

    g0 = c0 + 3 * CONV_CH
    sga_out[0] = jax.nn.sigmoid(part(g0, D_MODEL)).astype(BF16)
    sgb_out[0] = jax.nn.sigmoid(part(g0 + D_MODEL, D_MODEL)).astype(BF16)


def _proj(x, sh, sc, g1, w_in_bf, qg, kg, gmat, conv_w, conv_b, hist=None):
    nb, t, d = x.shape
    sample_mode = hist is not None
    tm = t if sample_mode else ROW_TILE
    nt = t // tm
    mod_rows = sh.shape[1]
    mod_block = (1, tm, d) if mod_rows == t else (1, 1, d)
    mod_map = (lambda b, i: (b, i, 0)) if mod_rows == t else (lambda b, i: (b, 0, 0))
    row = lambda w: pl.BlockSpec((1, tm, w), lambda b, i: (b, i, 0))
    const = lambda shape: _resident(shape, lambda b, i: (0,) * len(shape))

    in_specs = [row(d), pl.BlockSpec(mod_block, mod_map), pl.BlockSpec(mod_block, mod_map),
                const((1, d)), const((d, IN_WIDTH)), const((1, ATT_WIDTH)), const((1, ATT_WIDTH)),
                const((V7X_MXU_DIM, V7X_MXU_DIM)), const((CONV_WIDTH, CONV_CH)), const((1, CONV_CH))]
    args = [x, sh, sc, g1, w_in_bf, qg, kg, gmat, conv_w, conv_b]
    out_shape = [jax.ShapeDtypeStruct((nb, t, ATT_WIDTH), BF16),
                 jax.ShapeDtypeStruct((nb, t, ATT_WIDTH), F32),
                 jax.ShapeDtypeStruct((nb, t, ATT_WIDTH), BF16),
                 jax.ShapeDtypeStruct((nb, t, ATT_WIDTH), F32),
                 jax.ShapeDtypeStruct((nb, t, ATT_WIDTH), BF16),
                 jax.ShapeDtypeStruct((nb, t, CONV_CH), BF16),
                 jax.ShapeDtypeStruct((nb, t, D_MODEL), BF16),
                 jax.ShapeDtypeStruct((nb, t, D_MODEL), BF16)]
    out_specs = [row(ATT_WIDTH)] * 5 + [row(CONV_CH), row(D_MODEL), row(D_MODEL)]
    if sample_mode:
        in_specs += [const((2, t, CONV_CH)), const((2, t, CONV_CH))]
        args += list(hist)
        out_shape.append(jax.ShapeDtypeStruct((nb, t, CONV_CH), F32))
        out_specs.append(row(CONV_CH))
    else:
        out_shape.append(jax.ShapeDtypeStruct((nb, CONV_WIDTH - 1, CONV_CH), F32))
        out_specs.append(pl.BlockSpec((1, CONV_WIDTH - 1, CONV_CH), lambda b, i: (b, 0, 0)))

    est = (d * IN_WIDTH * 2 + 2 * tm * d * 4 + 2 * tm * (ATT_WIDTH * 14 + CONV_CH * 6 + D_MODEL * 4)
           + 6 * tm * D_MODEL * 4)
    return pl.pallas_call(
        functools.partial(_proj_kernel, tm=tm, sample_mode=sample_mode),
        grid=(nb, nt),
        in_specs=in_specs,
        out_specs=out_specs,
        out_shape=out_shape,
        scratch_shapes=[pltpu.VMEM((tm + HIST_ROWS, CONV_CH), F32)],
        compiler_params=pltpu.CompilerParams(
            dimension_semantics=("arbitrary", "arbitrary"),
            vmem_limit_bytes=_vmem_limit(est)),
        name="proj_sample" if sample_mode else "proj_prompt",
    )(*args)


def _attn_prompt_kernel(bias_ref, q_ref, k_ref, v_ref, u_ref, o_ref, qm_ref, z_buf, w_buf, carry_ref, acc_ref):
    i = pl.program_id(1)
    tq = q_ref.shape[1]
    pair = V7X_LANES
    n_pairs = q_ref.shape[2] // pair
    per_pair = pair // HEAD_DIM
    n_heads = n_pairs * per_pair
    lane_q = lax.broadcasted_iota(jnp.int32, (tq, pair), 1)
    lane_k = lax.broadcasted_iota(jnp.int32, (K_TILE, pair), 1)
    row = lax.broadcasted_iota(jnp.int32, (tq, K_TILE), 0)
    col = lax.broadcasted_iota(jnp.int32, (tq, K_TILE), 1)

    for hp in range(n_pairs):
        q2 = q_ref[0, :, hp * pair:(hp + 1) * pair]
        for hh in range(per_pair):
            qm_ref[hp * per_pair + hh] = jnp.where(lane_q // HEAD_DIM == hh, q2, jnp.zeros_like(q2))
    carry_ref[...] = jnp.zeros(carry_ref.shape, F32)
    acc_ref[...] = jnp.zeros(acc_ref.shape, F32)

    def key_rows(j):
        return pl.ds(pl.multiple_of(j * K_TILE, K_TILE), K_TILE)

    def scores(j, slot):
        for h in range(n_heads):
            hp = h // per_pair
            kt = k_ref[0, key_rows(j), hp * pair:(hp + 1) * pair]
            z = lax.dot_general(qm_ref[h], kt, (((1,), (1,)), ((), ())), preferred_element_type=F32)
            z_buf[slot, h] = jnp.minimum(z + bias_ref[h], SOFTPLUS2_LINEAR_FROM)

    def log_weights(slot, diagonal):
        for h in range(n_heads):
            z = z_buf[slot, h]
            sp = _softplus2(z)
            if diagonal:
                valid = row > col
                sp = jnp.where(valid, sp, 0.0)
            p = _mm(sp, u_ref[...])
            carry = carry_ref[h]
            w = z + p + jnp.concatenate([carry] * (K_TILE // pair), axis=1)
            if diagonal:
                w = jnp.where(valid, w, MASKED_LOG_WEIGHT)
            w_buf[slot, h] = w
            carry_ref[h] = carry + jnp.broadcast_to(p[:, 0:1], carry.shape)

    def weighted_values(j, slot, live):
        for hp in range(n_pairs):
            vt = v_ref[0, key_rows(j), hp * pair:(hp + 1) * pair]
            out = None
            for hh in range(per_pair):
                vh = jnp.where(lane_k // HEAD_DIM == hh, vt, jnp.zeros_like(vt))
                contrib = _mm(jnp.exp2(w_buf[slot, hp * per_pair + hh]), vh)
                out = contrib if out is None else out + contrib
            acc_ref[hp] += out if live is None else out * live

    scores(i, 0)
    log_weights(0, diagonal=True)
    scores(jnp.maximum(i - 1, 0), 1)

    def step(n, slot, live):
        weighted_values(jnp.maximum(i - n, 0), slot, live)
        log_weights(1 - slot, diagonal=False)
        scores(jnp.maximum(i - n - 2, 0), slot)

    def two_steps(m, _):
        n = 2 * m
        step(n, 0, None)
        step(n + 1, 1, (n + 1 <= i).astype(F32))
        return 0

    lax.fori_loop(0, (i + 2) // 2, two_steps, 0)
    for hp in range(n_pairs):
        o_ref[0, :, hp * pair:(hp + 1) * pair] = acc_ref[hp].astype(o_ref.dtype)


def _attn_prompt(q_bf, k_bf, v_bf, bias, umat):
    nb, t, width = q_bf.shape
    tq = Q_TILE
    pair = V7X_LANES
    tile_bytes = tq * K_TILE * 4
    scratch = [pltpu.VMEM((N_HEADS, tq, pair), BF16),
               pltpu.VMEM((2, N_HEADS, tq, K_TILE), F32),
               pltpu.VMEM((2, N_HEADS, tq, K_TILE), F32),
               pltpu.VMEM((N_HEADS, tq, pair), F32),
               pltpu.VMEM((width // pair, tq, pair), F32)]
    est = (2 * t * width * 2 + 4 * tq * width * 2 + 4 * N_HEADS * tile_bytes
           + N_HEADS * tq * pair * (2 + 4) + 4 * tq * pair * 4 + 16 * tile_bytes)
    return pl.pallas_call(
        _attn_prompt_kernel,
        grid=(nb, t // tq),
        in_specs=[pl.BlockSpec(memory_space=pltpu.SMEM),
                  pl.BlockSpec((1, tq, width), lambda b, i: (b, i, 0)),
                  _resident((1, t, width), lambda b, i: (b, 0, 0)),
                  _resident((1, t, width), lambda b, i: (b, 0, 0)),
                  _resident((K_TILE, K_TILE), lambda b, i: (0, 0))],
        out_specs=pl.BlockSpec((1, tq, width), lambda b, i: (b, i, 0)),
        out_shape=jax.ShapeDtypeStruct((nb, t, width), BF16),
        scratch_shapes=scratch,
        compiler_params=pltpu.CompilerParams(
            dimension_semantics=("arbitrary", "arbitrary"),
            vmem_limit_bytes=_vmem_limit(est)),
        name="attn_prompt",
    )(bias, q_bf, k_bf, v_bf, umat)


SAMPLE_ROWS = 32
QUAD = V7X_LANES // SAMPLE_ROWS


def _attn_sample_kernel(pt_ref, qbd_ref, bcol_ref, u_ref, knt_ref, vnt_ref, *rest, n_pages):
    del pt_ref
    kt_refs = rest[:n_pages]
    vt_refs = rest[n_pages:2 * n_pages]
    o_ref, acc_ref, carry_ref = rest[2 * n_pages:]
    c = pl.program_id(1)
    rows = SAMPLE_ROWS
    qbd = qbd_ref[0]
    bcol = bcol_ref[...]
    col_block = lax.broadcasted_iota(jnp.int32, (PAGE_SIZE, V7X_LANES), 1) // rows

    def scores(kt):
        z2 = jnp.dot(qbd, kt, preferred_element_type=F32) + bcol
        return jnp.minimum(z2, SOFTPLUS2_LINEAR_FROM)

    def suffix(sp):
        return jnp.dot(sp.astype(BF16), u_ref[...], preferred_element_type=F32)

    def transposed_weights(a_blocks):
        at = jnp.concatenate(a_blocks, axis=0).T
        return [jnp.where(col_block == p, at, 0.0) for p in range(len(a_blocks))]

    @pl.when(c == 0)
    def _():
        z = scores(knt_ref[0])
        key = lax.broadcasted_iota(jnp.int32, (rows, PAGE_SIZE), 1)
        qry = lax.broadcasted_iota(jnp.int32, (rows, PAGE_SIZE), 0) // N_HEADS
        valid = key < qry
        sp = jnp.where(valid, _softplus2(z), 0.0)
        p = suffix(sp)
        a = jnp.where(valid, jnp.exp2(z + p), 0.0)
        zero = jnp.zeros_like(a)
        w = transposed_weights([a] + [zero] * (QUAD - 1))[0]
        acc_ref[...] = jnp.dot(vnt_ref[0], w, preferred_element_type=F32)
        carry_ref[...] = p[:, 0:1]

    zs = [scores(kt_refs[g][0]) for g in range(n_pages)]
    z_all = jnp.concatenate(zs, axis=0)
    p_all = suffix(_softplus2(z_all))
    carry = carry_ref[...]
    carries = [None] * n_pages
    for g in reversed(range(n_pages)):
        carries[g] = carry
        carry = carry + p_all[g * rows:(g + 1) * rows, 0:1]
    carry_ref[...] = carry
    a_all = jnp.exp2(z_all + p_all + jnp.concatenate(carries, axis=0))

    acc = acc_ref[...]
    for q0 in range(0, n_pages, QUAD):
        w = transposed_weights([a_all[(q0 + p) * rows:(q0 + p + 1) * rows, :] for p in range(QUAD)])
        for p in range(0, QUAD, 2):
            vt2 = jnp.concatenate([vt_refs[q0 + p][0], vt_refs[q0 + p + 1][0]], axis=1)
            w2 = jnp.concatenate([w[p], w[p + 1]], axis=0)
            acc = acc + jnp.dot(vt2, w2, preferred_element_type=F32)
    acc_ref[...] = acc

    @pl.when(c == pl.num_programs(1) - 1)
    def _():
        total = acc
        for p in range(1, QUAD):
            total = total + pltpu.roll(acc, p * rows, axis=1)
        o_ref[0] = total


def _attn_sample(page_table, qbd, bcol, umat, knt, vnt, cache_kt, cache_vt):
    nb, n_tab = page_table.shape
    g = PAGES_PER_STEP
    n_chunks = n_tab // g
    rows = SAMPLE_ROWS
    assert qbd.shape[1] == rows and g % QUAD == 0
    page_block = (1, ATT_WIDTH, PAGE_SIZE)

    def page_spec(slot):
        return pl.BlockSpec(page_block, lambda b, c, pt: (pt[b, (n_chunks - 1 - c) * g + slot], 0, 0))

    per_seq = lambda shape: pl.BlockSpec(shape, lambda b, c, pt: (b, 0, 0))
    const = lambda shape: pl.BlockSpec(shape, lambda b, c, pt: (0,) * len(shape))
    in_specs = ([per_seq((1, rows, ATT_WIDTH)), const((rows, 1)), const((PAGE_SIZE, PAGE_SIZE)),
                 per_seq(page_block), per_seq(page_block)]
                + [page_spec(s) for s in range(g)] + [page_spec(s) for s in range(g)])
    grid_spec = pltpu.PrefetchScalarGridSpec(
        num_scalar_prefetch=1,
        grid=(nb, n_chunks),
        in_specs=in_specs,
        out_specs=pl.BlockSpec((1, ATT_WIDTH, V7X_LANES), lambda b, c, pt: (b, 0, 0)),
        scratch_shapes=[pltpu.VMEM((ATT_WIDTH, V7X_LANES), F32),
                        pltpu.VMEM((rows, 1), F32)])
    est = 2 * 2 * g * PAGE_SIZE * ATT_WIDTH * 4 + 8 * PAGE_SIZE * ATT_WIDTH * 4
    return pl.pallas_call(
        functools.partial(_attn_sample_kernel, n_pages=g),
        grid_spec=grid_spec,
        out_shape=jax.ShapeDtypeStruct((nb, ATT_WIDTH, V7X_LANES), F32),
        compiler_params=pltpu.CompilerParams(
            dimension_semantics=("arbitrary", "arbitrary"),
            vmem_limit_bytes=_vmem_limit(est)),
        name="attn_sample",
    )(page_table, qbd, bcol, umat, knt, vnt, *([cache_kt] * g), *([cache_vt] * g))


def _merge_kernel(x_ref, o_ref, yc_ref, sga_ref, sgb_ref, gt1_ref, sh2_ref, sc2_ref, gt2_ref, g2_ref,
                  wa_ref, wc_ref, wo_ref, w1_ref, w2_ref, out_ref):
    ya = jnp.dot(o_ref[0], wa_ref[...], preferred_element_type=F32)
    yb = jnp.dot(yc_ref[0], wc_ref[...], preferred_element_type=F32)
    mixed = (sga_ref[0].astype(F32) * ya + sgb_ref[0].astype(F32) * yb).astype(BF16)
    x1 = x_ref[0] + gt1_ref[0] * jnp.dot(mixed, wo_ref[...], preferred_element_type=F32)
    ms = jnp.mean(x1 * x1, axis=-1, keepdims=True)
    h2 = x1 * lax.rsqrt(ms + EPS) * g2_ref[...]
    h2 = (h2 * (1.0 + sc2_ref[0]) + sh2_ref[0]).astype(BF16)
    chunk = D_MODEL
    mlp = jnp.zeros(x1.shape, F32)
    for c in range(D_FF // chunk):
        hid = jnp.maximum(jnp.dot(h2, w1_ref[:, c * chunk:(c + 1) * chunk], preferred_element_type=F32), 0.0)
        hid = (hid * hid).astype(BF16)
        mlp = mlp + jnp.dot(hid, w2_ref[c * chunk:(c + 1) * chunk, :], preferred_element_type=F32)
    out_ref[0] = x1 + gt2_ref[0] * mlp


def _merge(x, o_att, yc, sga, sgb, gt1, sh2, sc2, gt2, g2, wa, wc, wo, w1, w2, tm):
    nb, t, d = x.shape
    nt = t // tm
    mod_rows = gt1.shape[1]
    mod_block = (1, tm, d) if mod_rows == t else (1, 1, d)
    mod_map = (lambda b, i: (b, i, 0)) if mod_rows == t else (lambda b, i: (b, 0, 0))
    row = lambda w: pl.BlockSpec((1, tm, w), lambda b, i: (b, i, 0))
    mod = pl.BlockSpec(mod_block, mod_map)
    const = lambda shape: _resident(shape, lambda b, i: (0,) * len(shape))
    w_bytes = 2 * (2 * ATT_WIDTH * d + d * d + 2 * d * D_FF)
    est = w_bytes + 2 * tm * (4 * d + 2 * ATT_WIDTH * 2 + 2 * d * 2 + 4 * d) + 8 * tm * d * 4
    return pl.pallas_call(
        _merge_kernel,
        grid=(nb, nt),
        in_specs=[row(d), row(ATT_WIDTH), row(CONV_CH), row(d), row(d), mod, mod, mod, mod, const((1, d)),
                  const((ATT_WIDTH, d)), const((CONV_CH, d)), const((d, d)), const((d, D_FF)), const((D_FF, d))],
        out_specs=row(d),
        out_shape=jax.ShapeDtypeStruct((nb, t, d), F32),
        compiler_params=pltpu.CompilerParams(
            dimension_semantics=("arbitrary", "arbitrary"),
            vmem_limit_bytes=_vmem_limit(est)),
        name="merge_mlp",
    )(x, o_att, yc, sga, sgb, gt1, sh2, sc2, gt2, g2, wa, wc, wo, w1, w2)


def _neg_suffix_matrix(n):
    j = jnp.arange(n)[:, None]
    s = jnp.arange(n)[None, :]
    return jnp.where(j >= s, -1.0, 0.0).astype(BF16)


def _slot_minor_pages(cache):
    n_phys, slots, heads, dim = cache.shape
    return jnp.transpose(cache, (0, 2, 3, 1)).reshape(n_phys, heads * dim, slots)


def kernel(x_prompt, x_sample, c_prompt, c_sample, cache_k, cache_v, state_conv, page_table, rms_g1, rms_g2,
           w_ada, b_ada, w_in, q_norm_g, k_norm_g, sb_bias, conv_w, conv_b, w_att_out, w_conv_out, w_o, w_mlp1,
           w_mlp2):
    depth = w_in.shape[0]
    assert depth == 1, "single-layer step"
    l = 0
    bp, seq, d = x_prompt.shape
    bs, dec, _ = x_sample.shape
    assert dec * N_HEADS == SAMPLE_ROWS

    w_in_bf = w_in[l].astype(BF16)
    wa, wc, wo = w_att_out[l].astype(BF16), w_conv_out[l].astype(BF16), w_o[l].astype(BF16)
    w1, w2 = w_mlp1[l].astype(BF16), w_mlp2[l].astype(BF16)
    g1 = rms_g1[l].reshape(1, d)
    g2 = rms_g2[l].reshape(1, d)
    qg = jnp.tile(q_norm_g[l], N_HEADS).reshape(1, ATT_WIDTH)
    kg = jnp.tile(k_norm_g[l], N_HEADS).reshape(1, ATT_WIDTH)
    cw = conv_w[l]
    cb = conv_b[l].reshape(1, CONV_CH)
    grp = jnp.arange(V7X_MXU_DIM) // HEAD_DIM
    gmat = jnp.where(grp[:, None] == grp[None, :], 1.0 / HEAD_DIM, 0.0).astype(BF16)
    bias = sb_bias[l].astype(F32) * LOG2E

    mod = _ada(jnp.concatenate([c_prompt, c_sample], axis=0), w_ada[l], b_ada[l])
    mods = [mod[:, j * d:(j + 1) * d] for j in range(N_MOD)]
    mp = [m[:bp].reshape(bp, 1, d) for m in mods]
    ms_ = [jnp.repeat(m[bp:], dec, axis=0).reshape(1, bs * dec, d) for m in mods]

    (q_p, k_p, kb_p, v_p, vb_p, yc_p, sga_p, sgb_p, cs_p) = _proj(
        x_prompt, mp[0], mp[1], g1, w_in_bf, qg, kg, gmat, cw, cb)
    o_p = _attn_prompt(q_p, kb_p, vb_p, bias, _neg_suffix_matrix(K_TILE))
    y_p = _merge(x_prompt, o_p, yc_p, sga_p, sgb_p, mp[2], mp[3], mp[4], mp[5], g2, wa, wc, wo, w1, w2,
                 tm=ROW_TILE)

    rows = bs * dec
    xs = x_sample.reshape(1, rows, d)
    st = state_conv[l]
    step = jnp.tile(jnp.arange(dec), bs)[:, None]
    flag1 = jnp.broadcast_to(step < 1, (rows, CONV_CH)).astype(F32)
    flag2 = jnp.broadcast_to(step < 2, (rows, CONV_CH)).astype(F32)
    zero = jnp.zeros((bs, dec - 2, CONV_CH), F32)
    val1 = jnp.concatenate([st[:, 1:2], zero, zero[:, :1]], axis=1).reshape(rows, CONV_CH)
    val2 = jnp.concatenate([st, zero], axis=1).reshape(rows, CONV_CH)
    hist = (jnp.stack([flag1, val1]), jnp.stack([flag2, val2]))
    (q_s, k_s, _, v_s, _, yc_s, sga_s, sgb_s, u_s) = _proj(
        xs, ms_[0], ms_[1], g1, w_in_bf, qg, kg, gmat, cw, cb, hist=hist)

    q4 = q_s.reshape(bs, dec, N_HEADS, HEAD_DIM)
    eye = jnp.eye(N_HEADS, dtype=BF16)
    qbd = jnp.einsum('bthd,hg->bthgd', q4, eye).reshape(bs, SAMPLE_ROWS, ATT_WIDTH)
    bcol = jnp.tile(bias, dec).reshape(SAMPLE_ROWS, 1)
    pad_slots = ((0, 0), (0, 0), (0, PAGE_SIZE - dec))
    knt = jnp.pad(jnp.transpose(k_s.reshape(bs, dec, ATT_WIDTH), (0, 2, 1)), pad_slots)
    vnt = jnp.pad(jnp.transpose(v_s.reshape(bs, dec, ATT_WIDTH), (0, 2, 1)), pad_slots)
    ot = _attn_sample(page_table, qbd, bcol, _neg_suffix_matrix(PAGE_SIZE), knt, vnt,
                      _slot_minor_pages(cache_k[l]), _slot_minor_pages(cache_v[l]))
    ot = ot[:, :, :SAMPLE_ROWS].reshape(bs, N_HEADS, HEAD_DIM, dec, N_HEADS)
    o_s = jnp.einsum('bhdth->bthd', ot).reshape(1, rows, ATT_WIDTH).astype(BF16)
    y_s = _merge(xs, o_s, yc_s, sga_s, sgb_s, ms_[2], ms_[3], ms_[4], ms_[5], g2, wa, wc, wo, w1, w2, tm=rows)

    heads = (N_HEADS, HEAD_DIM)
    return (y_p,
            y_s.reshape(bs, dec, d),
            k_p.reshape(1, bp, seq, *heads), v_p.reshape(1, bp, seq, *heads),
            cs_p.reshape(1, bp, CONV_WIDTH - 1, CONV_CH),
            k_s.reshape(1, bs, dec, *heads), v_s.reshape(1, bs, dec, *heads),
            u_s.reshape(bs, dec, CONV_CH)[:, dec - (CONV_WIDTH - 1):].reshape(1, bs, CONV_WIDTH - 1, CONV_CH))
```

```python
import functools
import math

import jax
import jax.numpy as jnp
from jax import lax
from jax.experimental import pallas as pl
from jax.experimental.pallas import tpu as pltpu

D_MODEL = 1024
N_HEADS = 8
HEAD_DIM = 64
ATT_WIDTH = N_HEADS * HEAD_DIM
CONV_CH = 512
CONV_WIDTH = 3
D_FF = 4 * D_MODEL
N_MOD = 6
PAGE_SIZE = 128
EPS = 1e-6
SB_SCALE = 1.0 / math.sqrt(HEAD_DIM)
LOG2E = math.log2(math.e)
Q_SCALE = SB_SCALE * LOG2E
IN_WIDTH = 3 * ATT_WIDTH + 3 * CONV_CH + 2 * D_MODEL

V7X_LANES = 128
V7X_SUBLANES = 8
V7X_MXU_DIM = 256
V7X_VMEM_BYTES = 64 * 1024 * 1024

ROW_TILE = 512
K_TILE = V7X_MXU_DIM
Q_TILE = K_TILE
PAGES_PER_STEP = 16
PAGE_BUFFERS = 2
HIST_ROWS = V7X_SUBLANES
SOFTPLUS2_LINEAR_FROM = 60.0
MASKED_LOG_WEIGHT = -1e30

F32 = jnp.float32
BF16 = jnp.bfloat16


def _vmem_limit(nbytes):
    return int(min(nbytes + (8 << 20), V7X_VMEM_BYTES - (6 << 20)))


def _resident(shape, index_map):
    return pl.BlockSpec(shape, index_map, pipeline_mode=pl.Buffered(1))


def _mm(lhs, rhs):
    return lax.dot_general(lhs, rhs, (((1,), (0,)), ((), ())), preferred_element_type=F32)


def _softplus2(z2):
    return jnp.log(1.0 + jnp.exp2(z2)) * LOG2E


def _ada_kernel(c_ref, w_ref, b_ref, o_ref):
    c = c_ref[...]
    s = (c * jax.nn.sigmoid(c)).astype(BF16)
    o_ref[...] = jnp.dot(s, w_ref[...], preferred_element_type=F32) + b_ref[...]


def _ada(c_all, w_ada, b_ada):
    n, d = c_all.shape
    width = w_ada.shape[1]
    tn = 1536
    return pl.pallas_call(
        _ada_kernel,
        grid=(width // tn,),
        in_specs=[pl.BlockSpec((n, d), lambda j: (0, 0)),
                  pl.BlockSpec((d, tn), lambda j: (0, j)),
                  pl.BlockSpec((1, tn), lambda j: (0, j))],
        out_specs=pl.BlockSpec((n, tn), lambda j: (0, j)),
        out_shape=jax.ShapeDtypeStruct((n, width), F32),
        compiler_params=pltpu.CompilerParams(
            dimension_semantics=("arbitrary",),
            vmem_limit_bytes=_vmem_limit(2 * d * tn * 4)),
        name="ada",
    )(c_all, w_ada, b_ada.reshape(1, width))


def _head_rms(p, gmat_ref, gain):
    sq = (p * p).astype(BF16)
    parts = []
    for c in range(ATT_WIDTH // V7X_MXU_DIM):
        sl = slice(c * V7X_MXU_DIM, (c + 1) * V7X_MXU_DIM)
        parts.append(jnp.dot(sq[:, sl], gmat_ref[...], preferred_element_type=F32))
    ms = jnp.concatenate(parts, axis=-1)
    return p * lax.rsqrt(ms + EPS) * gain


def _proj_kernel(*refs, tm, sample_mode):
    if sample_mode:
        (x_ref, sh_ref, sc_ref, g1_ref, w_ref, qg_ref, kg_ref, gmat_ref, cw_ref, cb_ref, h1_ref, h2_ref,
         q_out, k_out, kb_out, v_out, vb_out, yc_out, sga_out, sgb_out, u_out, conv_buf) = refs
    else:
        (x_ref, sh_ref, sc_ref, g1_ref, w_ref, qg_ref, kg_ref, gmat_ref, cw_ref, cb_ref,
         q_out, k_out, kb_out, v_out, vb_out, yc_out, sga_out, sgb_out, cs_out, conv_buf) = refs

    x = x_ref[0]
    ms = jnp.mean(x * x, axis=-1, keepdims=True)
    h = x * lax.rsqrt(ms + EPS) * g1_ref[...]
    h = (h * (1.0 + sc_ref[0]) + sh_ref[0]).astype(BF16)

    def part(lo, width):
        return jnp.dot(h, w_ref[:, lo:lo + width], preferred_element_type=F32)

    a = ATT_WIDTH
    qn = _head_rms(part(0, a), gmat_ref, qg_ref[...])
    q_out[0] = (qn * Q_SCALE).astype(BF16)
    kn = _head_rms(part(a, a), gmat_ref, kg_ref[...])
    k_out[0] = kn
    kb_out[0] = kn.astype(BF16)
    v = part(2 * a, a)
    v_out[0] = v
    parity = (lax.broadcasted_iota(jnp.int32, v.shape, 1) // HEAD_DIM) % 2
    for p in range(2):
        vb_out[0, :, p * a:(p + 1) * a] = jnp.where(parity == p, v, 0.0).astype(BF16)

    c0 = 3 * a
    hc = part(c0, CONV_CH)
    bgate = part(c0 + CONV_CH, CONV_CH)
    cgate = part(c0 + 2 * CONV_CH, CONV_CH)
    u = cgate * hc

    conv_buf[HIST_ROWS:HIST_ROWS + tm, :] = u
    if sample_mode:
        u_out[0] = u
        conv_buf[0:HIST_ROWS, :] = jnp.zeros((HIST_ROWS, CONV_CH), F32)
    else:
        @pl.when(pl.program_id(1) == 0)
        def _():
            conv_buf[0:HIST_ROWS, :] = jnp.zeros((HIST_ROWS, CONV_CH), F32)

    u1 = conv_buf[HIST_ROWS - 1:HIST_ROWS - 1 + tm, :]
    u2 = conv_buf[HIST_ROWS - 2:HIST_ROWS - 2 + tm, :]
    if sample_mode:
        u1 = jnp.where(h1_ref[0] != 0, h1_ref[1], u1)
        u2 = jnp.where(h2_ref[0] != 0, h2_ref[1], u2)
    y = cb_ref[...] + cw_ref[0:1, :] * u2 + cw_ref[1:2, :] * u1 + cw_ref[2:3, :] * u
    yc_out[0] = (bgate * y).astype(BF16)
    if not sample_mode:
        conv_buf[0:HIST_ROWS, :] = u[tm - HIST_ROWS:, :]

        @pl.when(pl.program_id(1) == pl.num_programs(1) - 1)
        def _():
            cs_out[0] = u[tm - (CONV_WIDTH - 1):, :]

    g0 = c0 + 3 * CONV_CH
    sga_out[0] = jax.nn.sigmoid(part(g0, D_MODEL)).astype(BF16)
    sgb_out[0] = jax.nn.sigmoid(part(g0 + D_MODEL, D_MODEL)).astype(BF16)


def _proj(x, sh, sc, g1, w_in_bf, qg, kg, gmat, conv_w, conv_b, hist=None):
    nb, t, d = x.shape
    sample_mode = hist is not None
    tm = t if sample_mode else ROW_TILE
    nt = t // tm
    mod_rows = sh.shape[1]
    mod_block = (1, tm, d) if mod_rows == t else (1, 1, d)
    mod_map = (lambda b, i: (b, i, 0)) if mod_rows == t else (lambda b, i: (b, 0, 0))
    row = lambda w: pl.BlockSpec((1, tm, w), lambda b, i: (b, i, 0))
    const = lambda shape: _resident(shape, lambda b, i: (0,) * len(shape))

    in_specs = [row(d), pl.BlockSpec(mod_block, mod_map), pl.BlockSpec(mod_block, mod_map),
                const((1, d)), const((d, IN_WIDTH)), const((1, ATT_WIDTH)), const((1, ATT_WIDTH)),
                const((V7X_MXU_DIM, V7X_MXU_DIM)), const((CONV_WIDTH, CONV_CH)), const((1, CONV_CH))]
    args = [x, sh, sc, g1, w_in_bf, qg, kg, gmat, conv_w, conv_b]
    out_shape = [jax.ShapeDtypeStruct((nb, t, ATT_WIDTH), BF16),
                 jax.ShapeDtypeStruct((nb, t, ATT_WIDTH), F32),
                 jax.ShapeDtypeStruct((nb, t, ATT_WIDTH), BF16),
                 jax.ShapeDtypeStruct((nb, t, ATT_WIDTH), F32),
                 jax.ShapeDtypeStruct((nb, t, 2 * ATT_WIDTH), BF16),
                 jax.ShapeDtypeStruct((nb, t, CONV_CH), BF16),
                 jax.ShapeDtypeStruct((nb, t, D_MODEL), BF16),
                 jax.ShapeDtypeStruct((nb, t, D_MODEL), BF16)]
    out_specs = [row(ATT_WIDTH)] * 4 + [row(2 * ATT_WIDTH), row(CONV_CH), row(D_MODEL), row(D_MODEL)]
    if sample_mode:
        in_specs += [const((2, t, CONV_CH)), const((2, t, CONV_CH))]
        args += list(hist)
        out_shape.append(jax.ShapeDtypeStruct((nb, t, CONV_CH), F32))
        out_specs.append(row(CONV_CH))
    else:
        out_shape.append(jax.ShapeDtypeStruct((nb, CONV_WIDTH - 1, CONV_CH), F32))
        out_specs.append(pl.BlockSpec((1, CONV_WIDTH - 1, CONV_CH), lambda b, i: (b, 0, 0)))

    est = (d * IN_WIDTH * 2 + 2 * tm * d * 4 + 2 * tm * (ATT_WIDTH * 16 + CONV_CH * 6 + D_MODEL * 4)
           + 6 * tm * D_MODEL * 4)
    return pl.pallas_call(
        functools.partial(_proj_kernel, tm=tm, sample_mode=sample_mode),
        grid=(nb, nt),
        in_specs=in_specs,
        out_specs=out_specs,
        out_shape=out_shape,
        scratch_shapes=[pltpu.VMEM((tm + HIST_ROWS, CONV_CH), F32)],
        compiler_params=pltpu.CompilerParams(
            dimension_semantics=("arbitrary", "arbitrary"),
            vmem_limit_bytes=_vmem_limit(est)),
        name="proj_sample" if sample_mode else "proj_prompt",
    )(*args)


def _attn_prompt_kernel(bias_ref, q_ref, k_ref, v_ref, u_ref, o_ref, qm_ref, z_buf, w_buf, carry_ref, acc_ref):
    i = pl.program_id(1)
    tq = q_ref.shape[1]
    pair = V7X_LANES
    n_pairs = q_ref.shape[2] // pair
    per_pair = pair // HEAD_DIM
    n_heads = n_pairs * per_pair
    lane_q = lax.broadcasted_iota(jnp.int32, (tq, pair), 1)
    row = lax.broadcasted_iota(jnp.int32, (tq, K_TILE), 0)
    col = lax.broadcasted_iota(jnp.int32, (tq, K_TILE), 1)

    for hp in range(n_pairs):
        q2 = q_ref[0, :, hp * pair:(hp + 1) * pair]
        for hh in range(per_pair):
            qm_ref[hp * per_pair + hh] = jnp.where(lane_q // HEAD_DIM == hh, q2, jnp.zeros_like(q2))
    carry_ref[...] = jnp.zeros(carry_ref.shape, F32)
    acc_ref[...] = jnp.zeros(acc_ref.shape, F32)

    def key_rows(j):
        return pl.ds(pl.multiple_of(j * K_TILE, K_TILE), K_TILE)

    pairs = range(n_pairs)

    def scores(j, slot, hps=pairs, hhs=range(per_pair)):
        for h in [hp * per_pair + hh for hp in hps for hh in hhs]:
            hp = h // per_pair
            kt = k_ref[0, key_rows(j), hp * pair:(hp + 1) * pair]
            z = lax.dot_general(qm_ref[h], kt, (((1,), (1,)), ((), ())), preferred_element_type=F32)
            z_buf[slot, h] = jnp.minimum(z + bias_ref[h], SOFTPLUS2_LINEAR_FROM)

    def log_weights(slot, diagonal, hps=pairs, hhs=range(per_pair)):
        for h in [hp * per_pair + hh for hp in hps for hh in hhs]:
            z = z_buf[slot, h]
            sp = _softplus2(z)
            if diagonal:
                valid = row > col
                sp = jnp.where(valid, sp, 0.0)
            p = _mm(sp, u_ref[...])
            carry = carry_ref[h]
            w = z + p + jnp.concatenate([carry] * (K_TILE // pair), axis=1)
            if diagonal:
                w = jnp.where(valid, w, MASKED_LOG_WEIGHT)
            w_buf[slot, h] = w
            carry_ref[h] = carry + jnp.broadcast_to(p[:, 0:1], carry.shape)

    def weighted_values(j, slot, live, hps=pairs):
        width = n_pairs * pair
        for hp in hps:
            out = None
            for hh in range(per_pair):
                vh = v_ref[0, key_rows(j), hh * width + hp * pair:hh * width + (hp + 1) * pair]
                contrib = _mm(jnp.exp2(w_buf[slot, hp * per_pair + hh]), vh)
                out = contrib if out is None else out + contrib
            acc_ref[hp] += out if live is None else out * live

    scores(i, 0)
    scores(jnp.maximum(i - 1, 0), 1)
    log_weights(0, diagonal=True)

    def step(n, slot, live):
        for hp in pairs:
            log_weights(1 - slot, False, [hp])
            scores(jnp.maximum(i - n - 2, 0), slot, [hp])
            weighted_values(jnp.maximum(i - n, 0), slot, live, [hp])

    def two_steps(m, _):
        n = 2 * m
        step(n, 0, None)
        step(n + 1, 1, (n + 1 <= i).astype(F32))
        return 0

    lax.fori_loop(0, (i + 2) // 2, two_steps, 0)
    for hp in range(n_pairs):
        o_ref[0, :, hp * pair:(hp + 1) * pair] = acc_ref[hp].astype(o_ref.dtype)


def _attn_prompt(q_bf, k_bf, v_bf, bias, umat):
    nb, t, width = q_bf.shape
    tq = Q_TILE
    pair = V7X_LANES
    tile_bytes = tq * K_TILE * 4
    scratch = [pltpu.VMEM((N_HEADS, tq, pair), BF16),
               pltpu.VMEM((2, N_HEADS, tq, K_TILE), F32),
               pltpu.VMEM((2, N_HEADS, tq, K_TILE), F32),
               pltpu.VMEM((N_HEADS, tq, pair), F32),
               pltpu.VMEM((width // pair, tq, pair), F32)]
    est = (3 * t * width * 2 + 4 * tq * width * 2 + 4 * N_HEADS * tile_bytes
           + N_HEADS * tq * pair * (2 + 4) + 4 * tq * pair * 4 + 16 * tile_bytes)
    return pl.pallas_call(
        _attn_prompt_kernel,
        grid=(nb, t // tq),
        in_specs=[pl.BlockSpec(memory_space=pltpu.SMEM),
                  pl.BlockSpec((1, tq, width), lambda b, i: (b, i, 0)),
                  _resident((1, t, width), lambda b, i: (b, 0, 0)),
                  _resident((1, t, 2 * width), lambda b, i: (b, 0, 0)),
                  _resident((K_TILE, K_TILE), lambda b, i: (0, 0))],
        out_specs=pl.BlockSpec((1, tq, width), lambda b, i: (b, i, 0)),
        out_shape=jax.ShapeDtypeStruct((nb, t, width), BF16),
        scratch_shapes=scratch,
        compiler_params=pltpu.CompilerParams(
            dimension_semantics=("arbitrary", "arbitrary"),
            vmem_limit_bytes=_vmem_limit(est)),
        name="attn_prompt",
    )(bias, q_bf, k_bf, v_bf, umat)


SAMPLE_ROWS = 32
QUAD = V7X_LANES // SAMPLE_ROWS


def _attn_sample_kernel(pt_ref, qbd_ref, bcol_ref, u_ref, knt_ref, vnt_ref, *rest, n_pages):
    del pt_ref
    kt_refs = rest[:n_pages]
    vt_refs = rest[n_pages:2 * n_pages]
    o_ref, acc_ref, carry_ref = rest[2 * n_pages:]
    c = pl.program_id(1)
    rows = SAMPLE_ROWS
    qbd = qbd_ref[0]
    bcol = bcol_ref[...]
    col_block = lax.broadcasted_iota(jnp.int32, (PAGE_SIZE, V7X_LANES), 1) // rows

    def scores(kt):
        z2 = jnp.dot(qbd, kt, preferred_element_type=F32) + bcol
        return jnp.minimum(z2, SOFTPLUS2_LINEAR_FROM)

    def suffix(sp):
        return jnp.dot(sp.astype(BF16), u_ref[...], preferred_element_type=F32)

    def transposed_weights(a_blocks):
        at = jnp.concatenate(a_blocks, axis=0).T
        return [jnp.where(col_block == p, at, 0.0) for p in range(len(a_blocks))]

    @pl.when(c == 0)
    def _():
        z = scores(knt_ref[0])
        key = lax.broadcasted_iota(jnp.int32, (rows, PAGE_SIZE), 1)
        qry = lax.broadcasted_iota(jnp.int32, (rows, PAGE_SIZE), 0) // N_HEADS
        valid = key < qry
        sp = jnp.where(valid, _softplus2(z), 0.0)
        p = suffix(sp)
        a = jnp.where(valid, jnp.exp2(z + p), 0.0)
        zero = jnp.zeros_like(a)
        w = transposed_weights([a] + [zero] * (QUAD - 1))[0]
        acc_ref[...] = jnp.dot(vnt_ref[0], w, preferred_element_type=F32)
        carry_ref[...] = p[:, 0:1]

    zs = [scores(kt_refs[g][0]) for g in range(n_pages)]
    z_all = jnp.concatenate(zs, axis=0)
    p_all = suffix(_softplus2(z_all))
    carry = carry_ref[...]
    carries = [None] * n_pages
    for g in reversed(range(n_pages)):
        carries[g] = carry
        carry = carry + p_all[g * rows:(g + 1) * rows, 0:1]
    carry_ref[...] = carry
    a_all = jnp.exp2(z_all + p_all + jnp.concatenate(carries, axis=0))

    acc = acc_ref[...]
    for q0 in range(0, n_pages, QUAD):
        w = transposed_weights([a_all[(q0 + p) * rows:(q0 + p + 1) * rows, :] for p in range(QUAD)])
        for p in range(0, QUAD, 2):
            vt2 = jnp.concatenate([vt_refs[q0 + p][0], vt_refs[q0 + p + 1][0]], axis=1)
            w2 = jnp.concatenate([w[p], w[p + 1]], axis=0)
            acc = acc + jnp.dot(vt2, w2, preferred_element_type=F32)
    acc_ref[...] = acc

    @pl.when(c == pl.num_programs(1) - 1)
    def _():
        total = acc
        for p in range(1, QUAD):
            total = total + pltpu.roll(acc, p * rows, axis=1)
        o_ref[0] = total


def _attn_sample(page_table, qbd, bcol, umat, knt, vnt, cache_kt, cache_vt):
    nb, n_tab = page_table.shape
    g = PAGES_PER_STEP
    n_chunks = n_tab // g
    rows = SAMPLE_ROWS
    assert qbd.shape[1] == rows and g % QUAD == 0
    page_block = (1, ATT_WIDTH, PAGE_SIZE)

    def page_spec(slot):
        return pl.BlockSpec(page_block, lambda b, c, pt: (pt[b, (n_chunks - 1 - c) * g + slot], 0, 0))

    per_seq = lambda shape: pl.BlockSpec(shape, lambda b, c, pt: (b, 0, 0))
    const = lambda shape: pl.BlockSpec(shape, lambda b, c, pt: (0,) * len(shape))
    in_specs = ([per_seq((1, rows, ATT_WIDTH)), const((rows, 1)), const((PAGE_SIZE, PAGE_SIZE)),
                 per_seq(page_block), per_seq(page_block)]
                + [page_spec(s) for s in range(g)] + [page_spec(s) for s in range(g)])
    grid_spec = pltpu.PrefetchScalarGridSpec(
        num_scalar_prefetch=1,
        grid=(nb, n_chunks),
        in_specs=in_specs,
        out_specs=pl.BlockSpec((1, ATT_WIDTH, V7X_LANES), lambda b, c, pt: (b, 0, 0)),
        scratch_shapes=[pltpu.VMEM((ATT_WIDTH, V7X_LANES), F32),
                        pltpu.VMEM((rows, 1), F32)])
    est = PAGE_BUFFERS * 2 * g * PAGE_SIZE * ATT_WIDTH * 4 + 8 * PAGE_SIZE * ATT_WIDTH * 4
    return pl.pallas_call(
        functools.partial(_attn_sample_kernel, n_pages=g),
        grid_spec=grid_spec,
        out_shape=jax.ShapeDtypeStruct((nb, ATT_WIDTH, V7X_LANES), F32),
        compiler_params=pltpu.CompilerParams(
            dimension_semantics=("arbitrary", "arbitrary"),
            vmem_limit_bytes=_vmem_limit(est)),
        name="attn_sample",
    )(page_table, qbd, bcol, umat, knt, vnt, *([cache_kt] * g), *([cache_vt] * g))


def _merge_kernel(x_ref, o_ref, yc_ref, sga_ref, sgb_ref, gt1_ref, sh2_ref, sc2_ref, gt2_ref, g2_ref,
                  wa_ref, wc_ref, wo_ref, w1_ref, w2_ref, out_ref):
    ya = jnp.dot(o_ref[0], wa_ref[...], preferred_element_type=F32)
    yb = jnp.dot(yc_ref[0], wc_ref[...], preferred_element_type=F32)
    mixed = (sga_ref[0].astype(F32) * ya + sgb_ref[0].astype(F32) * yb).astype(BF16)
    x1 = x_ref[0] + gt1_ref[0] * jnp.dot(mixed, wo_ref[...], preferred_element_type=F32)
    ms = jnp.mean(x1 * x1, axis=-1, keepdims=True)
    h2 = x1 * lax.rsqrt(ms + EPS) * g2_ref[...]
    h2 = (h2 * (1.0 + sc2_ref[0]) + sh2_ref[0]).astype(BF16)
    chunk = D_MODEL
    mlp = jnp.zeros(x1.shape, F32)
    for c in range(D_FF // chunk):
        hid = jnp.maximum(jnp.dot(h2, w1_ref[:, c * chunk:(c + 1) * chunk], preferred_element_type=F32), 0.0)
        hid = (hid * hid).astype(BF16)
        mlp = mlp + jnp.dot(hid, w2_ref[c * chunk:(c + 1) * chunk, :], preferred_element_type=F32)
    out_ref[0] = x1 + gt2_ref[0] * mlp


def _merge(x, o_att, yc, sga, sgb, gt1, sh2, sc2, gt2, g2, wa, wc, wo, w1, w2, tm):
    nb, t, d = x.shape
    nt = t // tm
    mod_rows = gt1.shape[1]
    mod_block = (1, tm, d) if mod_rows == t else (1, 1, d)
    mod_map = (lambda b, i: (b, i, 0)) if mod_rows == t else (lambda b, i: (b, 0, 0))
    row = lambda w: pl.BlockSpec((1, tm, w), lambda b, i: (b, i, 0))
    mod = pl.BlockSpec(mod_block, mod_map)
    const = lambda shape: _resident(shape, lambda b, i: (0,) * len(shape))
    w_bytes = 2 * (2 * ATT_WIDTH * d + d * d + 2 * d * D_FF)
    est = w_bytes + 2 * tm * (4 * d + 2 * ATT_WIDTH * 2 + 2 * d * 2 + 4 * d) + 8 * tm * d * 4
    return pl.pallas_call(
        _merge_kernel,
        grid=(nb, nt),
        in_specs=[row(d), row(ATT_WIDTH), row(CONV_CH), row(d), row(d), mod, mod, mod, mod, const((1, d)),
                  const((ATT_WIDTH, d)), const((CONV_CH, d)), const((d, d)), const((d, D_FF)), const((D_FF, d))],
        out_specs=row(d),
        out_shape=jax.ShapeDtypeStruct((nb, t, d), F32),
        compiler_params=pltpu.CompilerParams(
            dimension_semantics=("arbitrary", "arbitrary"),
            vmem_limit_bytes=_vmem_limit(est)),
        name="merge_mlp",
    )(x, o_att, yc, sga, sgb, gt1, sh2, sc2, gt2, g2, wa, wc, wo, w1, w2)


def _neg_suffix_matrix(n):
    j = jnp.arange(n)[:, None]
    s = jnp.arange(n)[None, :]
    return jnp.where(j >= s, -1.0, 0.0).astype(BF16)


def _slot_minor_pages(cache):
    n_phys, slots, heads, dim = cache.shape
    return jnp.transpose(cache, (0, 2, 3, 1)).reshape(n_phys, heads * dim, slots)


def kernel(x_prompt, x_sample, c_prompt, c_sample, cache_k, cache_v, state_conv, page_table, rms_g1, rms_g2,
           w_ada, b_ada, w_in, q_norm_g, k_norm_g, sb_bias, conv_w, conv_b, w_att_out, w_conv_out, w_o, w_mlp1,
           w_mlp2):
    depth = w_in.shape[0]
    assert depth == 1, "single-layer step"
    l = 0
    bp, seq, d = x_prompt.shape
    bs, dec, _ = x_sample.shape
    assert dec * N_HEADS == SAMPLE_ROWS

    w_in_bf = w_in[l].astype(BF16)
    wa, wc, wo = w_att_out[l].astype(BF16), w_conv_out[l].astype(BF16), w_o[l].astype(BF16)
    w1, w2 = w_mlp1[l].astype(BF16), w_mlp2[l].astype(BF16)
    g1 = rms_g1[l].reshape(1, d)
    g2 = rms_g2[l].reshape(1, d)
    qg = jnp.tile(q_norm_g[l], N_HEADS).reshape(1, ATT_WIDTH)
    kg = jnp.tile(k_norm_g[l], N_HEADS).reshape(1, ATT_WIDTH)
    cw = conv_w[l]
    cb = conv_b[l].reshape(1, CONV_CH)
    grp = jnp.arange(V7X_MXU_DIM) // HEAD_DIM
    gmat = jnp.where(grp[:, None] == grp[None, :], 1.0 / HEAD_DIM, 0.0).astype(BF16)
    bias = sb_bias[l].astype(F32) * LOG2E

    mod = _ada(jnp.concatenate([c_prompt, c_sample], axis=0), w_ada[l], b_ada[l])
    mods = [mod[:, j * d:(j + 1) * d] for j in range(N_MOD)]
    mp = [m[:bp].reshape(bp, 1, d) for m in mods]
    ms_ = [jnp.repeat(m[bp:], dec, axis=0).reshape(1, bs * dec, d) for m in mods]

    (q_p, k_p, kb_p, v_p, vb_p, yc_p, sga_p, sgb_p, cs_p) = _proj(
        x_prompt, mp[0], mp[1], g1, w_in_bf, qg, kg, gmat, cw, cb)
    o_p = _attn_prompt(q_p, kb_p, vb_p, bias, _neg_suffix_matrix(K_TILE))
    y_p = _merge(x_prompt, o_p, yc_p, sga_p, sgb_p, mp[2], mp[3], mp[4], mp[5], g2, wa, wc, wo, w1, w2,
                 tm=ROW_TILE)

    rows = bs * dec
    xs = x_sample.reshape(1, rows, d)
    st = state_conv[l]
    step = jnp.tile(jnp.arange(dec), bs)[:, None]
    flag1 = jnp.broadcast_to(step < 1, (rows, CONV_CH)).astype(F32)
    flag2 = jnp.broadcast_to(step < 2, (rows, CONV_CH)).astype(F32)
    zero = jnp.zeros((bs, dec - 2, CONV_CH), F32)
    val1 = jnp.concatenate([st[:, 1:2], zero, zero[:, :1]], axis=1).reshape(rows, CONV_CH)
    val2 = jnp.concatenate([st, zero], axis=1).reshape(rows, CONV_CH)
    hist = (jnp.stack([flag1, val1]), jnp.stack([flag2, val2]))
    (q_s, k_s, _, v_s, _, yc_s, sga_s, sgb_s, u_s) = _proj(
        xs, ms_[0], ms_[1], g1, w_in_bf, qg, kg, gmat, cw, cb, hist=hist)

    q4 = q_s.reshape(bs, dec, N_HEADS, HEAD_DIM)
    eye = jnp.eye(N_HEADS, dtype=BF16)
    qbd = jnp.einsum('bthd,hg->bthgd', q4, eye).reshape(bs, SAMPLE_ROWS, ATT_WIDTH)
    bcol = jnp.tile(bias, dec).reshape(SAMPLE_ROWS, 1)
    pad_slots = ((0, 0), (0, 0), (0, PAGE_SIZE - dec))
    knt = jnp.pad(jnp.transpose(k_s.reshape(bs, dec, ATT_WIDTH), (0, 2, 1)), pad_slots)
    vnt = jnp.pad(jnp.transpose(v_s.reshape(bs, dec, ATT_WIDTH), (0, 2, 1)), pad_slots)
    ot = _attn_sample(page_table, qbd, bcol, _neg_suffix_matrix(PAGE_SIZE), knt, vnt,
                      _slot_minor_pages(cache_k[l]), _slot_minor_pages(cache_v[l]))
    ot = ot[:, :, :SAMPLE_ROWS].reshape(bs, N_HEADS, HEAD_DIM, dec, N_HEADS)
    o_s = jnp.einsum('bhdth->bthd', ot).reshape(1, rows, ATT_WIDTH).astype(BF16)
    y_s = _merge(xs, o_s, yc_s, sga_s, sgb_s, ms_[2], ms_[3], ms_[4], ms_[5], g2, wa, wc, wo, w1, w2, tm=rows)

    heads = (N_HEADS, HEAD_DIM)
    return (y_p,
            y_s.reshape(bs, dec, d),
            k_p.reshape(1, bp, seq, *heads), v_p.reshape(1, bp, seq, *heads),
            cs_p.reshape(1, bp, CONV_WIDTH - 1, CONV_CH),
            k_s.reshape(1, bs, dec, *heads), v_s.reshape(1, bs, dec, *heads),
            u_s.reshape(bs, dec, CONV_CH)[:, dec - (CONV_WIDTH - 1):].reshape(1, bs, CONV_WIDTH - 1, CONV_CH))
```

```python
import functools
import math

import jax
import jax.numpy as jnp
from jax import lax
from jax.experimental import pallas as pl
from jax.experimental.pallas import tpu as pltpu

D_MODEL = 1024
N_HEADS = 8
HEAD_DIM = 64
ATT_WIDTH = N_HEADS * HEAD_DIM
CONV_CH = 512
CONV_WIDTH = 3
D_FF = 4 * D_MODEL
N_MOD = 6
PAGE_SIZE = 128
EPS = 1e-6
SB_SCALE = 1.0 / math.sqrt(HEAD_DIM)
LOG2E = math.log2(math.e)
Q_SCALE = SB_SCALE * LOG2E
IN_WIDTH = 3 * ATT_WIDTH + 3 * CONV_CH + 2 * D_MODEL

V7X_LANES = 128
V7X_SUBLANES = 8
V7X_MXU_DIM = 256
V7X_VMEM_BYTES = 64 * 1024 * 1024

ROW_TILE = 512
K_TILE = V7X_MXU_DIM
Q_TILE = K_TILE
PAGES_PER_STEP = 16
PAGE_BUFFERS = 2
HIST_ROWS = V7X_SUBLANES
SOFTPLUS2_LINEAR_FROM = 60.0
MASKED_LOG_WEIGHT = -1e30

F32 = jnp.float32
BF16 = jnp.bfloat16


def _vmem_limit(nbytes):
    return int(min(nbytes + (8 << 20), V7X_VMEM_BYTES - (6 << 20)))


def _resident(shape, index_map):
    return pl.BlockSpec(shape, index_map, pipeline_mode=pl.Buffered(1))


def _mm(lhs, rhs):
    return lax.dot_general(lhs, rhs, (((1,), (0,)), ((), ())), preferred_element_type=F32)


def _softplus2(z2):
    return jnp.log(1.0 + jnp.exp2(z2)) * LOG2E


def _ada_kernel(c_ref, w_ref, b_ref, o_ref):
    c = c_ref[...]
    s = (c * jax.nn.sigmoid(c)).astype(BF16)
    o_ref[...] = jnp.dot(s, w_ref[...], preferred_element_type=F32) + b_ref[...]


def _ada(c_all, w_ada, b_ada):
    n, d = c_all.shape
    width = w_ada.shape[1]
    tn = 1536
    return pl.pallas_call(
        _ada_kernel,
        grid=(width // tn,),
        in_specs=[pl.BlockSpec((n, d), lambda j: (0, 0)),
                  pl.BlockSpec((d, tn), lambda j: (0, j)),
                  pl.BlockSpec((1, tn), lambda j: (0, j))],
        out_specs=pl.BlockSpec((n, tn), lambda j: (0, j)),
        out_shape=jax.ShapeDtypeStruct((n, width), F32),
        compiler_params=pltpu.CompilerParams(
            dimension_semantics=("arbitrary",),
            vmem_limit_bytes=_vmem_limit(2 * d * tn * 4)),
        name="ada",
    )(c_all, w_ada, b_ada.reshape(1, width))


def _head_rms(p, gmat_ref, gain):
    sq = (p * p).astype(BF16)
    parts = []
    for c in range(ATT_WIDTH // V7X_MXU_DIM):
        sl = slice(c * V7X_MXU_DIM, (c + 1) * V7X_MXU_DIM)
        parts.append(jnp.dot(sq[:, sl], gmat_ref[...], preferred_element_type=F32))
    ms = jnp.concatenate(parts, axis=-1)
    return p * lax.rsqrt(ms + EPS) * gain


def _proj_kernel(*refs, tm, sample_mode):
    if sample_mode:
        (x_ref, sh_ref, sc_ref, g1_ref, w_ref, qg_ref, kg_ref, gmat_ref, cw_ref, cb_ref, h1_ref, h2_ref,
         q_out, k_out, kb_out, v_out, vb_out, yc_out, sga_out, sgb_out, u_out, conv_buf) = refs
    else:
        (x_ref, sh_ref, sc_ref, g1_ref, w_ref, qg_ref, kg_ref, gmat_ref, cw_ref, cb_ref,
         q_out, k_out, kb_out, v_out, vb_out, yc_out, sga_out, sgb_out, cs_out, conv_buf) = refs

    x = x_ref[0]
    ms = jnp.mean(x * x, axis=-1, keepdims=True)
    h = x * lax.rsqrt(ms + EPS) * g1_ref[...]
    h = (h * (1.0 + sc_ref[0]) + sh_ref[0]).astype(BF16)

    def part(lo, width):
        return jnp.dot(h, w_ref[:, lo:lo + width], preferred_element_type=F32)

    a = ATT_WIDTH
    qn = _head_rms(part(0, a), gmat_ref, qg_ref[...])
    q_out[0] = (qn * Q_SCALE).astype(BF16)
    kn = _head_rms(part(a, a), gmat_ref, kg_ref[...])
    k_out[0] = kn
    kb_out[0] = kn.astype(BF16)
    v = part(2 * a, a)
    v_out[0] = v
    parity = (lax.broadcasted_iota(jnp.int32, v.shape, 1) // HEAD_DIM) % 2
    for p in range(2):
        vb_out[0, :, p * a:(p + 1) * a] = jnp.where(parity == p, v, 0.0).astype(BF16)

    c0 = 3 * a
    hc = part(c0, CONV_CH)
    bgate = part(c0 + CONV_CH, CONV_CH)
    cgate = part(c0 + 2 * CONV_CH, CONV_CH)
    u = cgate * hc

    conv_buf[HIST_ROWS:HIST_ROWS + tm, :] = u
    if sample_mode:
        u_out[0] = u
        conv_buf[0:HIST_ROWS, :] = jnp.zeros((HIST_ROWS, CONV_CH), F32)
    else:
        @pl.when(pl.program_id(1) == 0)
        def _():
            conv_buf[0:HIST_ROWS, :] = jnp.zeros((HIST_ROWS, CONV_CH), F32)

    u1 = conv_buf[HIST_ROWS - 1:HIST_ROWS - 1 + tm, :]
    u2 = conv_buf[HIST_ROWS - 2:HIST_ROWS - 2 + tm, :]
    if sample_mode:
        u1 = jnp.where(h1_ref[0] != 0, h1_ref[1], u1)
        u2 = jnp.where(h2_ref[0] != 0, h2_ref[1], u2)
    y = cb_ref[...] + cw_ref[0:1, :] * u2 + cw_ref[1:2, :] * u1 + cw_ref[2:3, :] * u
    yc_out[0] = (bgate * y).astype(BF16)
    if not sample_mode:
        conv_buf[0:HIST_ROWS, :] = u[tm - HIST_ROWS:, :]

        @pl.when(pl.program_id(1) == pl.num_programs(1) - 1)
        def _():
            cs_out[0] = u[tm - (CONV_WIDTH - 1):, :]

    g0 = c0 + 3 * CONV_CH
    sga_out[0] = jax.nn.sigmoid(part(g0, D_MODEL)).astype(BF16)
    sgb_out[0] = jax.nn.sigmoid(part(g0 + D_MODEL, D_MODEL)).astype(BF16)


def _proj(x, sh, sc, g1, w_in_bf, qg, kg, gmat, conv_w, conv_b, hist=None):
    nb, t, d = x.shape
    sample_mode = hist is not None
    tm = t if sample_mode else ROW_TILE
    nt = t // tm
    mod_rows = sh.shape[1]
    mod_block = (1, tm, d) if mod_rows == t else (1, 1, d)
    mod_map = (lambda b, i: (b, i, 0)) if mod_rows == t else (lambda b, i: (b, 0, 0))
    row = lambda w: pl.BlockSpec((1, tm, w), lambda b, i: (b, i, 0))
    const = lambda shape: _resident(shape, lambda b, i: (0,) * len(shape))

    in_specs = [row(d), pl.BlockSpec(mod_block, mod_map), pl.BlockSpec(mod_block, mod_map),
                const((1, d)), const((d, IN_WIDTH)), const((1, ATT_WIDTH)), const((1, ATT_WIDTH)),
                const((V7X_MXU_DIM, V7X_MXU_DIM)), const((CONV_WIDTH, CONV_CH)), const((1, CONV_CH))]
    args = [x, sh, sc, g1, w_in_bf, qg, kg, gmat, conv_w, conv_b]
    out_shape = [jax.ShapeDtypeStruct((nb, t, ATT_WIDTH), BF16),
                 jax.ShapeDtypeStruct((nb, t, ATT_WIDTH), F32),
                 jax.ShapeDtypeStruct((nb, t, ATT_WIDTH), BF16),
                 jax.ShapeDtypeStruct((nb, t, ATT_WIDTH), F32),
                 jax.ShapeDtypeStruct((nb, t, 2 * ATT_WIDTH), BF16),
                 jax.ShapeDtypeStruct((nb, t, CONV_CH), BF16),
                 jax.ShapeDtypeStruct((nb, t, D_MODEL), BF16),
                 jax.ShapeDtypeStruct((nb, t, D_MODEL), BF16)]
    out_specs = [row(ATT_WIDTH)] * 4 + [row(2 * ATT_WIDTH), row(CONV_CH), row(D_MODEL), row(D_MODEL)]
    if sample_mode:
        in_specs += [const((2, t, CONV_CH)), const((2, t, CONV_CH))]
        args += list(hist)
        out_shape.append(jax.ShapeDtypeStruct((nb, t, CONV_CH), F32))
        out_specs.append(row(CONV_CH))
    else:
        out_shape.append(jax.ShapeDtypeStruct((nb, CONV_WIDTH - 1, CONV_CH), F32))
        out_specs.append(pl.BlockSpec((1, CONV_WIDTH - 1, CONV_CH), lambda b, i: (b, 0, 0)))

    est = (d * IN_WIDTH * 2 + 2 * tm * d * 4 + 2 * tm * (ATT_WIDTH * 16 + CONV_CH * 6 + D_MODEL * 4)
           + 6 * tm * D_MODEL * 4)
    return pl.pallas_call(
        functools.partial(_proj_kernel, tm=tm, sample_mode=sample_mode),
        grid=(nb, nt),
        in_specs=in_specs,
        out_specs=out_specs,
        out_shape=out_shape,
        scratch_shapes=[pltpu.VMEM((tm + HIST_ROWS, CONV_CH), F32)],
        compiler_params=pltpu.CompilerParams(
            dimension_semantics=("arbitrary", "arbitrary"),
            vmem_limit_bytes=_vmem_limit(est)),
        name="proj_sample" if sample_mode else "proj_prompt",
    )(*args)


def _prompt_attention_part(i, part, n_parts, bias_ref, q_ref, k_ref, v_ref, u_ref, o_ref,
                           qm_ref, z_buf, w_buf, carry_ref, acc_ref):
    tq = q_ref.shape[1]
    pair = V7X_LANES
    n_pairs = q_ref.shape[2] // pair
    per_pair = pair // HEAD_DIM

    def key_rows(j):
        return pl.ds(pl.multiple_of(j * K_TILE, K_TILE), K_TILE)

    pairs = range(n_pairs)

    def scores(j, slot, hps=pairs, hhs=range(per_pair)):
        for h in [hp * per_pair + hh for hp in hps for hh in hhs]:
            hp = h // per_pair
            kt = k_ref[0, key_rows(j), hp * pair:(hp + 1) * pair]
            z = lax.dot_general(qm_ref[h], kt, (((1,), (1,)), ((), ())), preferred_element_type=F32)
            z_buf[slot, h] = jnp.minimum(z + bias_ref[h], SOFTPLUS2_LINEAR_FROM)

    def log_weights(slot, diagonal, hps=pairs, hhs=range(per_pair)):
        for h in [hp * per_pair + hh for hp in hps for hh in hhs]:
            z = z_buf[slot, h]
            sp = _softplus2(z)
            if diagonal:
                row = lax.broadcasted_iota(jnp.int32, (tq, K_TILE), 0)
                col = lax.broadcasted_iota(jnp.int32, (tq, K_TILE), 1)
                valid = row > col
                sp = jnp.where(valid, sp, 0.0)
            p = _mm(sp, u_ref[...])
            carry = carry_ref[h]
            w = z + p + jnp.concatenate([carry] * (K_TILE // pair), axis=1)
            if diagonal:
                w = jnp.where(valid, w, MASKED_LOG_WEIGHT)
            w_buf[slot, h] = w
            carry_ref[h] = carry + jnp.broadcast_to(p[:, 0:1], carry.shape)

    def weighted_values(j, slot, live, hps=pairs):
        width = n_pairs * pair
        for hp in hps:
            out = None
            for hh in range(per_pair):
                vh = v_ref[0, key_rows(j), hh * width + hp * pair:hh * width + (hp + 1) * pair]
                contrib = _mm(jnp.exp2(w_buf[slot, hp * per_pair + hh]), vh)
                out = contrib if out is None else out + contrib
            acc_ref[hp] += out if live is None else out * live

    @pl.when(part == 0)
    def _():
        lane_q = lax.broadcasted_iota(jnp.int32, (tq, pair), 1)
        for hp in pairs:
            q2 = q_ref[0, :, hp * pair:(hp + 1) * pair]
            for hh in range(per_pair):
                qm_ref[hp * per_pair + hh] = jnp.where(lane_q // HEAD_DIM == hh, q2, jnp.zeros_like(q2))
        carry_ref[...] = jnp.zeros(carry_ref.shape, F32)
        acc_ref[...] = jnp.zeros(acc_ref.shape, F32)
        scores(i, 0)
        scores(jnp.maximum(i - 1, 0), 1)
        log_weights(0, diagonal=True)

    def step(n, slot, live):
        for hp in pairs:
            log_weights(1 - slot, False, [hp])
            scores(jnp.maximum(i - n - 2, 0), slot, [hp])
            weighted_values(jnp.maximum(i - n, 0), slot, live, [hp])

    def two_steps(m, _):
        n = 2 * m
        step(n, 0, None)
        step(n + 1, 1, (n + 1 <= i).astype(F32))
        return 0

    n_pairs_of_steps = (i + 2) // 2
    lax.fori_loop(n_pairs_of_steps * part // n_parts, n_pairs_of_steps * (part + 1) // n_parts, two_steps, 0)

    @pl.when(part == n_parts - 1)
    def _():
        for hp in pairs:
            o_ref[0, :, hp * pair:(hp + 1) * pair] = acc_ref[hp].astype(o_ref.dtype)


SAMPLE_ROWS = 32
QUAD = V7X_LANES // SAMPLE_ROWS


def _sample_attention_chunk(c, n_chunks, qbd_ref, bcol_ref, u_ref, knt_ref, vnt_ref, kt_refs, vt_refs,
                            o_ref, acc_ref, carry_ref):
    n_pages = len(kt_refs)
    rows = SAMPLE_ROWS
    qbd = qbd_ref[0]
    bcol = bcol_ref[...]
    col_block = lax.broadcasted_iota(jnp.int32, (PAGE_SIZE, V7X_LANES), 1) // rows

    def scores(kt):
        z2 = jnp.dot(qbd, kt, preferred_element_type=F32) + bcol
        return jnp.minimum(z2, SOFTPLUS2_LINEAR_FROM)

    def suffix(sp):
        return jnp.dot(sp.astype(BF16), u_ref[...], preferred_element_type=F32)

    def transposed_weights(a_blocks):
        at = jnp.concatenate(a_blocks, axis=0).T
        return [jnp.where(col_block == p, at, 0.0) for p in range(len(a_blocks))]

    @pl.when(c == 0)
    def _():
        z = scores(knt_ref[0])
        key = lax.broadcasted_iota(jnp.int32, (rows, PAGE_SIZE), 1)
        qry = lax.broadcasted_iota(jnp.int32, (rows, PAGE_SIZE), 0) // N_HEADS
        valid = key < qry
        sp = jnp.where(valid, _softplus2(z), 0.0)
        p = suffix(sp)
        a = jnp.where(valid, jnp.exp2(z + p), 0.0)
        zero = jnp.zeros_like(a)
        w = transposed_weights([a] + [zero] * (QUAD - 1))[0]
        acc_ref[...] = jnp.dot(vnt_ref[0], w, preferred_element_type=F32)
        carry_ref[...] = p[:, 0:1]

    zs = [scores(kt_refs[g][0]) for g in range(n_pages)]
    z_all = jnp.concatenate(zs, axis=0)
    p_all = suffix(_softplus2(z_all))
    carry = carry_ref[...]
    carries = [None] * n_pages
    for g in reversed(range(n_pages)):
        carries[g] = carry
        carry = carry + p_all[g * rows:(g + 1) * rows, 0:1]
    carry_ref[...] = carry
    a_all = jnp.exp2(z_all + p_all + jnp.concatenate(carries, axis=0))

    acc = acc_ref[...]
    for q0 in range(0, n_pages, QUAD):
        w = transposed_weights([a_all[(q0 + p) * rows:(q0 + p + 1) * rows, :] for p in range(QUAD)])
        for p in range(0, QUAD, 2):
            vt2 = jnp.concatenate([vt_refs[q0 + p][0], vt_refs[q0 + p + 1][0]], axis=1)
            w2 = jnp.concatenate([w[p], w[p + 1]], axis=0)
            acc = acc + jnp.dot(vt2, w2, preferred_element_type=F32)
    acc_ref[...] = acc

    @pl.when(c == n_chunks - 1)
    def _():
        total = acc
        for p in range(1, QUAD):
            total = total + pltpu.roll(acc, p * rows, axis=1)
        o_ref[0] = total


PROMPT_PARTS = 4


def _attention_kernel(pt_ref, bias_ref, q_ref, k_ref, v_ref, up_ref, qbd_ref, bcol_ref, us_ref, knt_ref, vnt_ref,
                      *rest, n_pages, n_chunks):
    del pt_ref
    kt_refs = rest[:n_pages]
    vt_refs = rest[n_pages:2 * n_pages]
    (o_ref, os_ref, qm_ref, z_buf, w_buf, carry_ref, acc_ref, s_acc_ref, s_carry_ref) = rest[2 * n_pages:]
    i = pl.program_id(1)
    part = pl.program_id(2)
    step = (pl.program_id(0) * pl.num_programs(1) + i) * PROMPT_PARTS + part
    _sample_attention_chunk(step % n_chunks, n_chunks, qbd_ref, bcol_ref, us_ref, knt_ref, vnt_ref,
                            kt_refs, vt_refs, os_ref, s_acc_ref, s_carry_ref)
    _prompt_attention_part(i, part, PROMPT_PARTS, bias_ref, q_ref, k_ref, v_ref, up_ref, o_ref,
                           qm_ref, z_buf, w_buf, carry_ref, acc_ref)


def _attention(q_bf, k_bf, v_bf2, bias, u_prompt, page_table, qbd, bcol, u_sample, knt, vnt, cache_kt, cache_vt):
    nb, t, width = q_bf.shape
    bs, n_tab = page_table.shape
    tq = Q_TILE
    nq = t // tq
    pair = V7X_LANES
    g = PAGES_PER_STEP
    n_chunks = n_tab // g
    rows = SAMPLE_ROWS
    assert qbd.shape[1] == rows and g % QUAD == 0
    assert nb * nq * PROMPT_PARTS == bs * n_chunks, "one page chunk per grid step"
    page_block = (1, ATT_WIDTH, PAGE_SIZE)

    def seq_chunk(b, i, c):
        step = (b * nq + i) * PROMPT_PARTS + c
        return step // n_chunks, step % n_chunks

    def page_spec(slot):
        def index(b, i, c, pt, bias_smem):
            seq, chunk = seq_chunk(b, i, c)
            return pt[seq, (n_chunks - 1 - chunk) * g + slot], 0, 0
        return pl.BlockSpec(page_block, index)

    per_seq = lambda shape: pl.BlockSpec(shape, lambda b, i, c, pt, bs_: (seq_chunk(b, i, c)[0], 0, 0))
    const = lambda shape: pl.BlockSpec(shape, lambda b, i, c, pt, bs_: (0,) * len(shape))
    in_specs = ([pl.BlockSpec((1, tq, width), lambda b, i, c, pt, bs_: (b, i, 0)),
                 _resident((1, t, width), lambda b, i, c, pt, bs_: (b, 0, 0)),
                 _resident((1, t, 2 * width), lambda b, i, c, pt, bs_: (b, 0, 0)),
                 _resident((K_TILE, K_TILE), lambda b, i, c, pt, bs_: (0, 0)),
                 per_seq((1, rows, ATT_WIDTH)), const((rows, 1)), const((PAGE_SIZE, PAGE_SIZE)),
                 per_seq(page_block), per_seq(page_block)]
                + [page_spec(s) for s in range(g)] + [page_spec(s) for s in range(g)])
    tile_bytes = tq * K_TILE * 4
    scratch = [pltpu.VMEM((N_HEADS, tq, pair), BF16),
               pltpu.VMEM((2, N_HEADS, tq, K_TILE), F32),
               pltpu.VMEM((2, N_HEADS, tq, K_TILE), F32),
               pltpu.VMEM((N_HEADS, tq, pair), F32),
               pltpu.VMEM((width // pair, tq, pair), F32),
               pltpu.VMEM((ATT_WIDTH, V7X_LANES), F32),
               pltpu.VMEM((rows, 1), F32)]
    grid_spec = pltpu.PrefetchScalarGridSpec(
        num_scalar_prefetch=2,
        grid=(nb, nq, PROMPT_PARTS),
        in_specs=in_specs,
        out_specs=[pl.BlockSpec((1, tq, width), lambda b, i, c, pt, bs_: (b, i, 0)),
                   pl.BlockSpec((1, ATT_WIDTH, V7X_LANES), lambda b, i, c, pt, bs_: (seq_chunk(b, i, c)[0], 0, 0))],
        scratch_shapes=scratch)
    est = (3 * t * width * 2 + 4 * tq * width * 2 + 4 * N_HEADS * tile_bytes
           + N_HEADS * tq * pair * (2 + 4) + 4 * tq * pair * 4 + 8 * tile_bytes
           + PAGE_BUFFERS * 2 * g * PAGE_SIZE * ATT_WIDTH * 4 + 6 * PAGE_SIZE * ATT_WIDTH * 4)
    return pl.pallas_call(
        functools.partial(_attention_kernel, n_pages=g, n_chunks=n_chunks),
        grid_spec=grid_spec,
        out_shape=[jax.ShapeDtypeStruct((nb, t, width), BF16),
                   jax.ShapeDtypeStruct((bs, ATT_WIDTH, V7X_LANES), F32)],
        compiler_params=pltpu.CompilerParams(
            dimension_semantics=("arbitrary", "arbitrary", "arbitrary"),
            vmem_limit_bytes=_vmem_limit(est)),
        name="attention",
    )(page_table, bias, q_bf, k_bf, v_bf2, u_prompt, qbd, bcol, u_sample, knt, vnt,
      *([cache_kt] * g), *([cache_vt] * g))


def _merge_kernel(x_ref, o_ref, yc_ref, sga_ref, sgb_ref, gt1_ref, sh2_ref, sc2_ref, gt2_ref, g2_ref,
                  wa_ref, wc_ref, wo_ref, w1_ref, w2_ref, out_ref):
    ya = jnp.dot(o_ref[0], wa_ref[...], preferred_element_type=F32)
    yb = jnp.dot(yc_ref[0], wc_ref[...], preferred_element_type=F32)
    mixed = (sga_ref[0].astype(F32) * ya + sgb_ref[0].astype(F32) * yb).astype(BF16)
    x1 = x_ref[0] + gt1_ref[0] * jnp.dot(mixed, wo_ref[...], preferred_element_type=F32)
    ms = jnp.mean(x1 * x1, axis=-1, keepdims=True)
    h2 = x1 * lax.rsqrt(ms + EPS) * g2_ref[...]
    h2 = (h2 * (1.0 + sc2_ref[0]) + sh2_ref[0]).astype(BF16)
    chunk = D_MODEL
    mlp = jnp.zeros(x1.shape, F32)
    for c in range(D_FF // chunk):
        hid = jnp.maximum(jnp.dot(h2, w1_ref[:, c * chunk:(c + 1) * chunk], preferred_element_type=F32), 0.0)
        hid = (hid * hid).astype(BF16)
        mlp = mlp + jnp.dot(hid, w2_ref[c * chunk:(c + 1) * chunk, :], preferred_element_type=F32)
    out_ref[0] = x1 + gt2_ref[0] * mlp


def _merge(x, o_att, yc, sga, sgb, gt1, sh2, sc2, gt2, g2, wa, wc, wo, w1, w2, tm):
    nb, t, d = x.shape
    nt = t // tm
    mod_rows = gt1.shape[1]
    mod_block = (1, tm, d) if mod_rows == t else (1, 1, d)
    mod_map = (lambda b, i: (b, i, 0)) if mod_rows == t else (lambda b, i: (b, 0, 0))
    row = lambda w: pl.BlockSpec((1, tm, w), lambda b, i: (b, i, 0))
    mod = pl.BlockSpec(mod_block, mod_map)
    const = lambda shape: _resident(shape, lambda b, i: (0,) * len(shape))
    w_bytes = 2 * (2 * ATT_WIDTH * d + d * d + 2 * d * D_FF)
    est = w_bytes + 2 * tm * (4 * d + 2 * ATT_WIDTH * 2 + 2 * d * 2 + 4 * d) + 8 * tm * d * 4
    return pl.pallas_call(
        _merge_kernel,
        grid=(nb, nt),
        in_specs=[row(d), row(ATT_WIDTH), row(CONV_CH), row(d), row(d), mod, mod, mod, mod, const((1, d)),
                  const((ATT_WIDTH, d)), const((CONV_CH, d)), const((d, d)), const((d, D_FF)), const((D_FF, d))],
        out_specs=row(d),
        out_shape=jax.ShapeDtypeStruct((nb, t, d), F32),
        compiler_params=pltpu.CompilerParams(
            dimension_semantics=("arbitrary", "arbitrary"),
            vmem_limit_bytes=_vmem_limit(est)),
        name="merge_mlp",
    )(x, o_att, yc, sga, sgb, gt1, sh2, sc2, gt2, g2, wa, wc, wo, w1, w2)


def _neg_suffix_matrix(n):
    j = jnp.arange(n)[:, None]
    s = jnp.arange(n)[None, :]
    return jnp.where(j >= s, -1.0, 0.0).astype(BF16)


def _slot_minor_pages(cache):
    n_phys, slots, heads, dim = cache.shape
    return jnp.transpose(cache, (0, 2, 3, 1)).reshape(n_phys, heads * dim, slots)


def kernel(x_prompt, x_sample, c_prompt, c_sample, cache_k, cache_v, state_conv, page_table, rms_g1, rms_g2,
           w_ada, b_ada, w_in, q_norm_g, k_norm_g, sb_bias, conv_w, conv_b, w_att_out, w_conv_out, w_o, w_mlp1,
           w_mlp2):
    depth = w_in.shape[0]
    assert depth == 1, "single-layer step"
    l = 0
    bp, seq, d = x_prompt.shape
    bs, dec, _ = x_sample.shape
    assert dec * N_HEADS == SAMPLE_ROWS

    w_in_bf = w_in[l].astype(BF16)
    wa, wc, wo = w_att_out[l].astype(BF16), w_conv_out[l].astype(BF16), w_o[l].astype(BF16)
    w1, w2 = w_mlp1[l].astype(BF16), w_mlp2[l].astype(BF16)
    g1 = rms_g1[l].reshape(1, d)
    g2 = rms_g2[l].reshape(1, d)
    qg = jnp.tile(q_norm_g[l], N_HEADS).reshape(1, ATT_WIDTH)
    kg = jnp.tile(k_norm_g[l], N_HEADS).reshape(1, ATT_WIDTH)
    cw = conv_w[l]
    cb = conv_b[l].reshape(1, CONV_CH)
    grp = jnp.arange(V7X_MXU_DIM) // HEAD_DIM
    gmat = jnp.where(grp[:, None] == grp[None, :], 1.0 / HEAD_DIM, 0.0).astype(BF16)
    bias = sb_bias[l].astype(F32) * LOG2E

    mod = _ada(jnp.concatenate([c_prompt, c_sample], axis=0), w_ada[l], b_ada[l])
    mods = [mod[:, j * d:(j + 1) * d] for j in range(N_MOD)]
    mp = [m[:bp].reshape(bp, 1, d) for m in mods]
    ms_ = [jnp.repeat(m[bp:], dec, axis=0).reshape(1, bs * dec, d) for m in mods]

    (q_p, k_p, kb_p, v_p, vb_p, yc_p, sga_p, sgb_p, cs_p) = _proj(
        x_prompt, mp[0], mp[1], g1, w_in_bf, qg, kg, gmat, cw, cb)

    rows = bs * dec
    xs = x_sample.reshape(1, rows, d)
    st = state_conv[l]
    step = jnp.tile(jnp.arange(dec), bs)[:, None]
    flag1 = jnp.broadcast_to(step < 1, (rows, CONV_CH)).astype(F32)
    flag2 = jnp.broadcast_to(step < 2, (rows, CONV_CH)).astype(F32)
    zero = jnp.zeros((bs, dec - 2, CONV_CH), F32)
    val1 = jnp.concatenate([st[:, 1:2], zero, zero[:, :1]], axis=1).reshape(rows, CONV_CH)
    val2 = jnp.concatenate([st, zero], axis=1).reshape(rows, CONV_CH)
    hist = (jnp.stack([flag1, val1]), jnp.stack([flag2, val2]))
    (q_s, k_s, _, v_s, _, yc_s, sga_s, sgb_s, u_s) = _proj(
        xs, ms_[0], ms_[1], g1, w_in_bf, qg, kg, gmat, cw, cb, hist=hist)

    q4 = q_s.reshape(bs, dec, N_HEADS, HEAD_DIM)
    eye = jnp.eye(N_HEADS, dtype=BF16)
    qbd = jnp.einsum('bthd,hg->bthgd', q4, eye).reshape(bs, SAMPLE_ROWS, ATT_WIDTH)
    bcol = jnp.tile(bias, dec).reshape(SAMPLE_ROWS, 1)
    pad_slots = ((0, 0), (0, 0), (0, PAGE_SIZE - dec))
    knt = jnp.pad(jnp.transpose(k_s.reshape(bs, dec, ATT_WIDTH), (0, 2, 1)), pad_slots)
    vnt = jnp.pad(jnp.transpose(v_s.reshape(bs, dec, ATT_WIDTH), (0, 2, 1)), pad_slots)
    o_p, ot = _attention(q_p, kb_p, vb_p, bias, _neg_suffix_matrix(K_TILE),
                         page_table, qbd, bcol, _neg_suffix_matrix(PAGE_SIZE), knt, vnt,
                         _slot_minor_pages(cache_k[l]), _slot_minor_pages(cache_v[l]))
    y_p = _merge(x_prompt, o_p, yc_p, sga_p, sgb_p, mp[2], mp[3], mp[4], mp[5], g2, wa, wc, wo, w1, w2,
                 tm=ROW_TILE)
    ot = ot[:, :, :SAMPLE_ROWS].reshape(bs, N_HEADS, HEAD_DIM, dec, N_HEADS)
    o_s = jnp.einsum('bhdth->bthd', ot).reshape(1, rows, ATT_WIDTH).astype(BF16)
    y_s = _merge(xs, o_s, yc_s, sga_s, sgb_s, ms_[2], ms_[3], ms_[4], ms_[5], g2, wa, wc, wo, w1, w2, tm=rows)

    heads = (N_HEADS, HEAD_DIM)
    return (y_p,
            y_s.reshape(bs, dec, d),
            k_p.reshape(1, bp, seq, *heads), v_p.reshape(1, bp, seq, *heads),
            cs_p.reshape(1, bp, CONV_WIDTH - 1, CONV_CH),
            k_s.reshape(1, bs, dec, *heads), v_s.reshape(1, bs, dec, *heads),
            u_s.reshape(bs, dec, CONV_CH)[:, dec - (CONV_WIDTH - 1):].reshape(1, bs, CONV_WIDTH - 1, CONV_CH))
```

```python
import functools
import math

import jax
import jax.numpy as jnp
from jax import lax
from jax.experimental import pallas as pl
from jax.experimental.pallas import tpu as pltpu

D_MODEL = 1024
N_HEADS = 8
HEAD_DIM = 64
ATT_WIDTH = N_HEADS * HEAD_DIM
CONV_CH = 512
CONV_WIDTH = 3
D_FF = 4 * D_MODEL
N_MOD = 6
PAGE_SIZE = 128
EPS = 1e-6
SB_SCALE = 1.0 / math.sqrt(HEAD_DIM)
LOG2E = math.log2(math.e)
Q_SCALE = SB_SCALE * LOG2E
IN_WIDTH = 3 * ATT_WIDTH + 3 * CONV_CH + 2 * D_MODEL

V7X_LANES = 128
V7X_SUBLANES = 8
V7X_MXU_DIM = 256
V7X_VMEM_BYTES = 64 * 1024 * 1024

ROW_TILE = 512
K_TILE = V7X_MXU_DIM
Q_TILE = K_TILE
PAGES_PER_STEP = 16
PAGE_BUFFERS = 2
HIST_ROWS = V7X_SUBLANES
SOFTPLUS2_LINEAR_FROM = 60.0
MASKED_LOG_WEIGHT = -1e30

F32 = jnp.float32
BF16 = jnp.bfloat16


def _vmem_limit(nbytes):
    return int(min(nbytes + (8 << 20), V7X_VMEM_BYTES - (6 << 20)))


def _resident(shape, index_map):
    return pl.BlockSpec(shape, index_map, pipeline_mode=pl.Buffered(1))


def _mm(lhs, rhs):
    return lax.dot_general(lhs, rhs, (((1,), (0,)), ((), ())), preferred_element_type=F32)


def _softplus2(z2):
    return jnp.log(1.0 + jnp.exp2(z2)) * LOG2E


def _ada_kernel(c_ref, w_ref, b_ref, o_ref):
    c = c_ref[...]
    s = (c * jax.nn.sigmoid(c)).astype(BF16)
    o_ref[...] = jnp.dot(s, w_ref[...], preferred_element_type=F32) + b_ref[...]


def _ada(c_all, w_ada, b_ada):
    n, d = c_all.shape
    width = w_ada.shape[1]
    tn = 1536
    return pl.pallas_call(
        _ada_kernel,
        grid=(width // tn,),
        in_specs=[pl.BlockSpec((n, d), lambda j: (0, 0)),
                  pl.BlockSpec((d, tn), lambda j: (0, j)),
                  pl.BlockSpec((1, tn), lambda j: (0, j))],
        out_specs=pl.BlockSpec((n, tn), lambda j: (0, j)),
        out_shape=jax.ShapeDtypeStruct((n, width), F32),
        compiler_params=pltpu.CompilerParams(
            dimension_semantics=("arbitrary",),
            vmem_limit_bytes=_vmem_limit(2 * d * tn * 4)),
        name="ada",
    )(c_all, w_ada, b_ada.reshape(1, width))


def _head_rms(p, gmat_ref, gain):
    sq = (p * p).astype(BF16)
    parts = []
    for c in range(ATT_WIDTH // V7X_MXU_DIM):
        sl = slice(c * V7X_MXU_DIM, (c + 1) * V7X_MXU_DIM)
        parts.append(jnp.dot(sq[:, sl], gmat_ref[...], preferred_element_type=F32))
    ms = jnp.concatenate(parts, axis=-1)
    return p * lax.rsqrt(ms + EPS) * gain


def _proj_kernel(*refs, tm, sample_mode):
    if sample_mode:
        (x_ref, sh_ref, sc_ref, g1_ref, w_ref, qg_ref, kg_ref, gmat_ref, cw_ref, cb_ref, h1_ref, h2_ref,
         q_out, k_out, kb_out, v_out, vb_out, yc_out, sga_out, sgb_out, u_out, conv_buf) = refs
    else:
        (x_ref, sh_ref, sc_ref, g1_ref, w_ref, qg_ref, kg_ref, gmat_ref, cw_ref, cb_ref,
         q_out, k_out, kb_out, v_out, vb_out, yc_out, sga_out, sgb_out, cs_out, conv_buf) = refs

    x = x_ref[0]
    ms = jnp.mean(x * x, axis=-1, keepdims=True)
    h = x * lax.rsqrt(ms + EPS) * g1_ref[...]
    h = (h * (1.0 + sc_ref[0]) + sh_ref[0]).astype(BF16)

    def part(lo, width):
        return jnp.dot(h, w_ref[:, lo:lo + width], preferred_element_type=F32)

    a = ATT_WIDTH
    qn = _head_rms(part(0, a), gmat_ref, qg_ref[...])
    q_out[0] = (qn * Q_SCALE).astype(BF16)
    kn = _head_rms(part(a, a), gmat_ref, kg_ref[...])
    k_out[0] = kn
    kb_out[0] = kn.astype(BF16)
    v = part(2 * a, a)
    v_out[0] = v
    parity = (lax.broadcasted_iota(jnp.int32, v.shape, 1) // HEAD_DIM) % 2
    for p in range(2):
        vb_out[0, :, p * a:(p + 1) * a] = jnp.where(parity == p, v, 0.0).astype(BF16)

    c0 = 3 * a
    hc = part(c0, CONV_CH)
    bgate = part(c0 + CONV_CH, CONV_CH)
    cgate = part(c0 + 2 * CONV_CH, CONV_CH)
    u = cgate * hc

    conv_buf[HIST_ROWS:HIST_ROWS + tm, :] = u
    if sample_mode:
        u_out[0] = u
        conv_buf[0:HIST_ROWS, :] = jnp.zeros((HIST_ROWS, CONV_CH), F32)
    else:
        @pl.when(pl.program_id(1) == 0)
        def _():
            conv_buf[0:HIST_ROWS, :] = jnp.zeros((HIST_ROWS, CONV_CH), F32)

    u1 = conv_buf[HIST_ROWS - 1:HIST_ROWS - 1 + tm, :]
    u2 = conv_buf[HIST_ROWS - 2:HIST_ROWS - 2 + tm, :]
    if sample_mode:
        u1 = jnp.where(h1_ref[0] != 0, h1_ref[1], u1)
        u2 = jnp.where(h2_ref[0] != 0, h2_ref[1], u2)
    y = cb_ref[...] + cw_ref[0:1, :] * u2 + cw_ref[1:2, :] * u1 + cw_ref[2:3, :] * u
    yc_out[0] = (bgate * y).astype(BF16)
    if not sample_mode:
        conv_buf[0:HIST_ROWS, :] = u[tm - HIST_ROWS:, :]

        @pl.when(pl.program_id(1) == pl.num_programs(1) - 1)
        def _():
            cs_out[0] = u[tm - (CONV_WIDTH - 1):, :]

    g0 = c0 + 3 * CONV_CH
    sga_out[0] = jax.nn.sigmoid(part(g0, D_MODEL)).astype(BF16)
    sgb_out[0] = jax.nn.sigmoid(part(g0 + D_MODEL, D_MODEL)).astype(BF16)


def _proj(x, sh, sc, g1, w_in_bf, qg, kg, gmat, conv_w, conv_b, hist=None):
    nb, t, d = x.shape
    sample_mode = hist is not None
    tm = t if sample_mode else ROW_TILE
    nt = t // tm
    mod_rows = sh.shape[1]
    mod_block = (1, tm, d) if mod_rows == t else (1, 1, d)
    mod_map = (lambda b, i: (b, i, 0)) if mod_rows == t else (lambda b, i: (b, 0, 0))
    row = lambda w: pl.BlockSpec((1, tm, w), lambda b, i: (b, i, 0))
    const = lambda shape: _resident(shape, lambda b, i: (0,) * len(shape))

    in_specs = [row(d), pl.BlockSpec(mod_block, mod_map), pl.BlockSpec(mod_block, mod_map),
                const((1, d)), const((d, IN_WIDTH)), const((1, ATT_WIDTH)), const((1, ATT_WIDTH)),
                const((V7X_MXU_DIM, V7X_MXU_DIM)), const((CONV_WIDTH, CONV_CH)), const((1, CONV_CH))]
    args = [x, sh, sc, g1, w_in_bf, qg, kg, gmat, conv_w, conv_b]
    out_shape = [jax.ShapeDtypeStruct((nb, t, ATT_WIDTH), BF16),
                 jax.ShapeDtypeStruct((nb, t, ATT_WIDTH), F32),
                 jax.ShapeDtypeStruct((nb, t, ATT_WIDTH), BF16),
                 jax.ShapeDtypeStruct((nb, t, ATT_WIDTH), F32),
                 jax.ShapeDtypeStruct((nb, t, 2 * ATT_WIDTH), BF16),
                 jax.ShapeDtypeStruct((nb, t, CONV_CH), BF16),
                 jax.ShapeDtypeStruct((nb, t, D_MODEL), BF16),
                 jax.ShapeDtypeStruct((nb, t, D_MODEL), BF16)]
    out_specs = [row(ATT_WIDTH)] * 4 + [row(2 * ATT_WIDTH), row(CONV_CH), row(D_MODEL), row(D_MODEL)]
    if sample_mode:
        in_specs += [const((2, t, CONV_CH)), const((2, t, CONV_CH))]
        args += list(hist)
        out_shape.append(jax.ShapeDtypeStruct((nb, t, CONV_CH), F32))
        out_specs.append(row(CONV_CH))
    else:
        out_shape.append(jax.ShapeDtypeStruct((nb, CONV_WIDTH - 1, CONV_CH), F32))
        out_specs.append(pl.BlockSpec((1, CONV_WIDTH - 1, CONV_CH), lambda b, i: (b, 0, 0)))

    est = (d * IN_WIDTH * 2 + 2 * tm * d * 4 + 2 * tm * (ATT_WIDTH * 16 + CONV_CH * 6 + D_MODEL * 4)
           + 6 * tm * D_MODEL * 4)
    return pl.pallas_call(
        functools.partial(_proj_kernel, tm=tm, sample_mode=sample_mode),
        grid=(nb, nt),
        in_specs=in_specs,
        out_specs=out_specs,
        out_shape=out_shape,
        scratch_shapes=[pltpu.VMEM((tm + HIST_ROWS, CONV_CH), F32)],
        compiler_params=pltpu.CompilerParams(
            dimension_semantics=("arbitrary", "arbitrary"),
            vmem_limit_bytes=_vmem_limit(est)),
        name="proj_sample" if sample_mode else "proj_prompt",
    )(*args)


def _prompt_attention_part(i, part, n_parts, bias_ref, q_ref, k_ref, v_ref, u_ref, o_ref,
                           qm_ref, z_buf, w_buf, carry_ref, acc_ref):
    tq = q_ref.shape[1]
    pair = V7X_LANES
    n_pairs = q_ref.shape[2] // pair
    per_pair = pair // HEAD_DIM

    def key_rows(j):
        return pl.ds(pl.multiple_of(j * K_TILE, K_TILE), K_TILE)

    pairs = range(n_pairs)

    def scores(j, slot, hps=pairs, hhs=range(per_pair)):
        for h in [hp * per_pair + hh for hp in hps for hh in hhs]:
            hp = h // per_pair
            kt = k_ref[0, key_rows(j), hp * pair:(hp + 1) * pair]
            z = lax.dot_general(qm_ref[h], kt, (((1,), (1,)), ((), ())), preferred_element_type=F32)
            z_buf[slot, h] = jnp.minimum(z + bias_ref[h], SOFTPLUS2_LINEAR_FROM)

    def log_weights(slot, diagonal, hps=pairs, hhs=range(per_pair)):
        for h in [hp * per_pair + hh for hp in hps for hh in hhs]:
            z = z_buf[slot, h]
            sp = _softplus2(z)
            if diagonal:
                row = lax.broadcasted_iota(jnp.int32, (tq, K_TILE), 0)
                col = lax.broadcasted_iota(jnp.int32, (tq, K_TILE), 1)
                valid = row > col
                sp = jnp.where(valid, sp, 0.0)
            p = _mm(sp, u_ref[...])
            carry = carry_ref[h]
            w = z + p + jnp.concatenate([carry] * (K_TILE // pair), axis=1)
            if diagonal:
                w = jnp.where(valid, w, MASKED_LOG_WEIGHT)
            w_buf[slot, h] = w
            carry_ref[h] = carry + jnp.broadcast_to(p[:, 0:1], carry.shape)

    def weighted_values(j, slot, hps=pairs):
        width = n_pairs * pair
        for hp in hps:
            out = None
            for hh in range(per_pair):
                vh = v_ref[0, key_rows(j), hh * width + hp * pair:hh * width + (hp + 1) * pair]
                contrib = _mm(jnp.exp2(w_buf[slot, hp * per_pair + hh]), vh)
                out = contrib if out is None else out + contrib
            acc_ref[hp] += out

    @pl.when(part == 0)
    def _():
        lane_q = lax.broadcasted_iota(jnp.int32, (tq, pair), 1)
        for hp in pairs:
            q2 = q_ref[0, :, hp * pair:(hp + 1) * pair]
            for hh in range(per_pair):
                qm_ref[hp * per_pair + hh] = jnp.where(lane_q // HEAD_DIM == hh, q2, jnp.zeros_like(q2))
        carry_ref[...] = jnp.zeros(carry_ref.shape, F32)
        acc_ref[...] = jnp.zeros(acc_ref.shape, F32)
        scores(i, 0)
        scores(jnp.maximum(i - 1, 0), 1)
        log_weights(0, diagonal=True)

    def step(n, slot):
        for hp in pairs:
            log_weights(1 - slot, False, [hp])
            scores(i - n - 2, slot, [hp])
            weighted_values(i - n, slot, [hp])

    def two_steps(m, _):
        step(2 * m, 0)
        step(2 * m + 1, 1)
        return 0

    n_full = jnp.maximum(i - 1, 0)
    n_loop = n_full // 2
    lax.fori_loop(n_loop * part // n_parts, n_loop * (part + 1) // n_parts, two_steps, 0)

    @pl.when(part == n_parts - 1)
    def _():
        @pl.when(n_full % 2 == 1)
        def _():
            step(n_full - 1, 0)

        @pl.when(i >= 1)
        def _():
            slot = (i - 1) % 2
            weighted_values(1, slot)
            log_weights(1 - slot, False)

        weighted_values(0, i % 2)
        for hp in pairs:
            o_ref[0, :, hp * pair:(hp + 1) * pair] = acc_ref[hp].astype(o_ref.dtype)


SAMPLE_ROWS = 32
QUAD = V7X_LANES // SAMPLE_ROWS


def _sample_attention_chunk(c, n_chunks, qbd_ref, bcol_ref, u_ref, knt_ref, vnt_ref, kt_refs, vt_refs,
                            o_ref, acc_ref, carry_ref):
    n_pages = len(kt_refs)
    rows = SAMPLE_ROWS
    qbd = qbd_ref[0]
    bcol = bcol_ref[...]
    col_block = lax.broadcasted_iota(jnp.int32, (PAGE_SIZE, V7X_LANES), 1) // rows

    def scores(kt):
        z2 = jnp.dot(qbd, kt, preferred_element_type=F32) + bcol
        return jnp.minimum(z2, SOFTPLUS2_LINEAR_FROM)

    def suffix(sp):
        return jnp.dot(sp.astype(BF16), u_ref[...], preferred_element_type=F32)

    def transposed_weights(a_blocks):
        at = jnp.concatenate(a_blocks, axis=0).T
        return [jnp.where(col_block == p, at, 0.0) for p in range(len(a_blocks))]

    @pl.when(c == 0)
    def _():
        z = scores(knt_ref[0])
        key = lax.broadcasted_iota(jnp.int32, (rows, PAGE_SIZE), 1)
        qry = lax.broadcasted_iota(jnp.int32, (rows, PAGE_SIZE), 0) // N_HEADS
        valid = key < qry
        sp = jnp.where(valid, _softplus2(z), 0.0)
        p = suffix(sp)
        a = jnp.where(valid, jnp.exp2(z + p), 0.0)
        zero = jnp.zeros_like(a)
        w = transposed_weights([a] + [zero] * (QUAD - 1))[0]
        acc_ref[...] = jnp.dot(vnt_ref[0], w, preferred_element_type=F32)
        carry_ref[...] = p[:, 0:1]

    zs = [scores(kt_refs[g][0]) for g in range(n_pages)]
    z_all = jnp.concatenate(zs, axis=0)
    p_all = suffix(_softplus2(z_all))
    carry = carry_ref[...]
    carries = [None] * n_pages
    for g in reversed(range(n_pages)):
        carries[g] = carry
        carry = carry + p_all[g * rows:(g + 1) * rows, 0:1]
    carry_ref[...] = carry
    a_all = jnp.exp2(z_all + p_all + jnp.concatenate(carries, axis=0))

    acc = acc_ref[...]
    for q0 in range(0, n_pages, QUAD):
        w = transposed_weights([a_all[(q0 + p) * rows:(q0 + p + 1) * rows, :] for p in range(QUAD)])
        for p in range(0, QUAD, 2):
            vt2 = jnp.concatenate([vt_refs[q0 + p][0], vt_refs[q0 + p + 1][0]], axis=1)
            w2 = jnp.concatenate([w[p], w[p + 1]], axis=0)
            acc = acc + jnp.dot(vt2, w2, preferred_element_type=F32)
    acc_ref[...] = acc

    @pl.when(c == n_chunks - 1)
    def _():
        total = acc
        for p in range(1, QUAD):
            total = total + pltpu.roll(acc, p * rows, axis=1)
        o_ref[0] = total


PROMPT_PARTS = 4


def _attention_kernel(pt_ref, bias_ref, q_ref, k_ref, v_ref, up_ref, qbd_ref, bcol_ref, us_ref, knt_ref, vnt_ref,
                      *rest, n_pages, n_chunks):
    del pt_ref
    kt_refs = rest[:n_pages]
    vt_refs = rest[n_pages:2 * n_pages]
    (o_ref, os_ref, qm_ref, z_buf, w_buf, carry_ref, acc_ref, s_acc_ref, s_carry_ref) = rest[2 * n_pages:]
    i = pl.program_id(1)
    part = pl.program_id(2)
    step = (pl.program_id(0) * pl.num_programs(1) + i) * PROMPT_PARTS + part
    _sample_attention_chunk(step % n_chunks, n_chunks, qbd_ref, bcol_ref, us_ref, knt_ref, vnt_ref,
                            kt_refs, vt_refs, os_ref, s_acc_ref, s_carry_ref)
    _prompt_attention_part(i, part, PROMPT_PARTS, bias_ref, q_ref, k_ref, v_ref, up_ref, o_ref,
                           qm_ref, z_buf, w_buf, carry_ref, acc_ref)


def _attention(q_bf, k_bf, v_bf2, bias, u_prompt, page_table, qbd, bcol, u_sample, knt, vnt, cache_kt, cache_vt):
    nb, t, width = q_bf.shape
    bs, n_tab = page_table.shape
    tq = Q_TILE
    nq = t // tq
    pair = V7X_LANES
    g = PAGES_PER_STEP
    n_chunks = n_tab // g
    rows = SAMPLE_ROWS
    assert qbd.shape[1] == rows and g % QUAD == 0
    assert nb * nq * PROMPT_PARTS == bs * n_chunks, "one page chunk per grid step"
    page_block = (1, ATT_WIDTH, PAGE_SIZE)

    def seq_chunk(b, i, c):
        step = (b * nq + i) * PROMPT_PARTS + c
        return step // n_chunks, step % n_chunks

    def page_spec(slot):
        def index(b, i, c, pt, bias_smem):
            seq, chunk = seq_chunk(b, i, c)
            return pt[seq, (n_chunks - 1 - chunk) * g + slot], 0, 0
        return pl.BlockSpec(page_block, index)

    per_seq = lambda shape: pl.BlockSpec(shape, lambda b, i, c, pt, bs_: (seq_chunk(b, i, c)[0], 0, 0))
    const = lambda shape: pl.BlockSpec(shape, lambda b, i, c, pt, bs_: (0,) * len(shape))
    in_specs = ([pl.BlockSpec((1, tq, width), lambda b, i, c, pt, bs_: (b, i, 0)),
                 _resident((1, t, width), lambda b, i, c, pt, bs_: (b, 0, 0)),
                 _resident((1, t, 2 * width), lambda b, i, c, pt, bs_: (b, 0, 0)),
                 _resident((K_TILE, K_TILE), lambda b, i, c, pt, bs_: (0, 0)),
                 per_seq((1, rows, ATT_WIDTH)), const((rows, 1)), const((PAGE_SIZE, PAGE_SIZE)),
                 per_seq(page_block), per_seq(page_block)]
                + [page_spec(s) for s in range(g)] + [page_spec(s) for s in range(g)])
    tile_bytes = tq * K_TILE * 4
    scratch = [pltpu.VMEM((N_HEADS, tq, pair), BF16),
               pltpu.VMEM((2, N_HEADS, tq, K_TILE), F32),
               pltpu.VMEM((2, N_HEADS, tq, K_TILE), F32),
               pltpu.VMEM((N_HEADS, tq, pair), F32),
               pltpu.VMEM((width // pair, tq, pair), F32),
               pltpu.VMEM((ATT_WIDTH, V7X_LANES), F32),
               pltpu.VMEM((rows, 1), F32)]
    grid_spec = pltpu.PrefetchScalarGridSpec(
        num_scalar_prefetch=2,
        grid=(nb, nq, PROMPT_PARTS),
        in_specs=in_specs,
        out_specs=[pl.BlockSpec((1, tq, width), lambda b, i, c, pt, bs_: (b, i, 0)),
                   pl.BlockSpec((1, ATT_WIDTH, V7X_LANES), lambda b, i, c, pt, bs_: (seq_chunk(b, i, c)[0], 0, 0))],
        scratch_shapes=scratch)
    est = (3 * t * width * 2 + 4 * tq * width * 2 + 4 * N_HEADS * tile_bytes
           + N_HEADS * tq * pair * (2 + 4) + 4 * tq * pair * 4 + 8 * tile_bytes
           + PAGE_BUFFERS * 2 * g * PAGE_SIZE * ATT_WIDTH * 4 + 6 * PAGE_SIZE * ATT_WIDTH * 4)
    return pl.pallas_call(
        functools.partial(_attention_kernel, n_pages=g, n_chunks=n_chunks),
        grid_spec=grid_spec,
        out_shape=[jax.ShapeDtypeStruct((nb, t, width), BF16),
                   jax.ShapeDtypeStruct((bs, ATT_WIDTH, V7X_LANES), F32)],
        compiler_params=pltpu.CompilerParams(
            dimension_semantics=("arbitrary", "arbitrary", "arbitrary"),
            vmem_limit_bytes=_vmem_limit(est)),
        name="attention",
    )(page_table, bias, q_bf, k_bf, v_bf2, u_prompt, qbd, bcol, u_sample, knt, vnt,
      *([cache_kt] * g), *([cache_vt] * g))


def _merge_kernel(x_ref, o_ref, yc_ref, sga_ref, sgb_ref, gt1_ref, sh2_ref, sc2_ref, gt2_ref, g2_ref,
                  wa_ref, wc_ref, wo_ref, w1_ref, w2_ref, out_ref):
    ya = jnp.dot(o_ref[0], wa_ref[...], preferred_element_type=F32)
    yb = jnp.dot(yc_ref[0], wc_ref[...], preferred_element_type=F32)
    mixed = (sga_ref[0].astype(F32) * ya + sgb_ref[0].astype(F32) * yb).astype(BF16)
    x1 = x_ref[0] + gt1_ref[0] * jnp.dot(mixed, wo_ref[...], preferred_element_type=F32)
    ms = jnp.mean(x1 * x1, axis=-1, keepdims=True)
    h2 = x1 * lax.rsqrt(ms + EPS) * g2_ref[...]
    h2 = (h2 * (1.0 + sc2_ref[0]) + sh2_ref[0]).astype(BF16)
    chunk = D_MODEL
    mlp = jnp.zeros(x1.shape, F32)
    for c in range(D_FF // chunk):
        hid = jnp.maximum(jnp.dot(h2, w1_ref[:, c * chunk:(c + 1) * chunk], preferred_element_type=F32), 0.0)
        hid = (hid * hid).astype(BF16)
        mlp = mlp + jnp.dot(hid, w2_ref[c * chunk:(c + 1) * chunk, :], preferred_element_type=F32)
    out_ref[0] = x1 + gt2_ref[0] * mlp


def _merge(x, o_att, yc, sga, sgb, gt1, sh2, sc2, gt2, g2, wa, wc, wo, w1, w2, tm):
    nb, t, d = x.shape
    nt = t // tm
    mod_rows = gt1.shape[1]
    mod_block = (1, tm, d) if mod_rows == t else (1, 1, d)
    mod_map = (lambda b, i: (b, i, 0)) if mod_rows == t else (lambda b, i: (b, 0, 0))
    row = lambda w: pl.BlockSpec((1, tm, w), lambda b, i: (b, i, 0))
    mod = pl.BlockSpec(mod_block, mod_map)
    const = lambda shape: _resident(shape, lambda b, i: (0,) * len(shape))
    w_bytes = 2 * (2 * ATT_WIDTH * d + d * d + 2 * d * D_FF)
    est = w_bytes + 2 * tm * (4 * d + 2 * ATT_WIDTH * 2 + 2 * d * 2 + 4 * d) + 8 * tm * d * 4
    return pl.pallas_call(
        _merge_kernel,
        grid=(nb, nt),
        in_specs=[row(d), row(ATT_WIDTH), row(CONV_CH), row(d), row(d), mod, mod, mod, mod, const((1, d)),
                  const((ATT_WIDTH, d)), const((CONV_CH, d)), const((d, d)), const((d, D_FF)), const((D_FF, d))],
        out_specs=row(d),
        out_shape=jax.ShapeDtypeStruct((nb, t, d), F32),
        compiler_params=pltpu.CompilerParams(
            dimension_semantics=("arbitrary", "arbitrary"),
            vmem_limit_bytes=_vmem_limit(est)),
        name="merge_mlp",
    )(x, o_att, yc, sga, sgb, gt1, sh2, sc2, gt2, g2, wa, wc, wo, w1, w2)


def _neg_suffix_matrix(n):
    j = jnp.arange(n)[:, None]
    s = jnp.arange(n)[None, :]
    return jnp.where(j >= s, -1.0, 0.0).astype(BF16)


def _slot_minor_pages(cache):
    n_phys, slots, heads, dim = cache.shape
    return jnp.transpose(cache, (0, 2, 3, 1)).reshape(n_phys, heads * dim, slots)


def kernel(x_prompt, x_sample, c_prompt, c_sample, cache_k, cache_v, state_conv, page_table, rms_g1, rms_g2,
           w_ada, b_ada, w_in, q_norm_g, k_norm_g, sb_bias, conv_w, conv_b, w_att_out, w_conv_out, w_o, w_mlp1,
           w_mlp2):
    depth = w_in.shape[0]
    assert depth == 1, "single-layer step"
    l = 0
    bp, seq, d = x_prompt.shape
    bs, dec, _ = x_sample.shape
    assert dec * N_HEADS == SAMPLE_ROWS

    w_in_bf = w_in[l].astype(BF16)
    wa, wc, wo = w_att_out[l].astype(BF16), w_conv_out[l].astype(BF16), w_o[l].astype(BF16)
    w1, w2 = w_mlp1[l].astype(BF16), w_mlp2[l].astype(BF16)
    g1 = rms_g1[l].reshape(1, d)
    g2 = rms_g2[l].reshape(1, d)
    qg = jnp.tile(q_norm_g[l], N_HEADS).reshape(1, ATT_WIDTH)
    kg = jnp.tile(k_norm_g[l], N_HEADS).reshape(1, ATT_WIDTH)
    cw = conv_w[l]
    cb = conv_b[l].reshape(1, CONV_CH)
    grp = jnp.arange(V7X_MXU_DIM) // HEAD_DIM
    gmat = jnp.where(grp[:, None] == grp[None, :], 1.0 / HEAD_DIM, 0.0).astype(BF16)
    bias = sb_bias[l].astype(F32) * LOG2E

    mod = _ada(jnp.concatenate([c_prompt, c_sample], axis=0), w_ada[l], b_ada[l])
    mods = [mod[:, j * d:(j + 1) * d] for j in range(N_MOD)]
    mp = [m[:bp].reshape(bp, 1, d) for m in mods]
    ms_ = [jnp.repeat(m[bp:], dec, axis=0).reshape(1, bs * dec, d) for m in mods]

    (q_p, k_p, kb_p, v_p, vb_p, yc_p, sga_p, sgb_p, cs_p) = _proj(
        x_prompt, mp[0], mp[1], g1, w_in_bf, qg, kg, gmat, cw, cb)

    rows = bs * dec
    xs = x_sample.reshape(1, rows, d)
    st = state_conv[l]
    step = jnp.tile(jnp.arange(dec), bs)[:, None]
    flag1 = jnp.broadcast_to(step < 1, (rows, CONV_CH)).astype(F32)
    flag2 = jnp.broadcast_to(step < 2, (rows, CONV_CH)).astype(F32)
    zero = jnp.zeros((bs, dec - 2, CONV_CH), F32)
    val1 = jnp.concatenate([st[:, 1:2], zero, zero[:, :1]], axis=1).reshape(rows, CONV_CH)
    val2 = jnp.concatenate([st, zero], axis=1).reshape(rows, CONV_CH)
    hist = (jnp.stack([flag1, val1]), jnp.stack([flag2, val2]))
    (q_s, k_s, _, v_s, _, yc_s, sga_s, sgb_s, u_s) = _proj(
        xs, ms_[0], ms_[1], g1, w_in_bf, qg, kg, gmat, cw, cb, hist=hist)

    q4 = q_s.reshape(bs, dec, N_HEADS, HEAD_DIM)
    eye = jnp.eye(N_HEADS, dtype=BF16)
    qbd = jnp.einsum('bthd,hg->bthgd', q4, eye).reshape(bs, SAMPLE_ROWS, ATT_WIDTH)
    bcol = jnp.tile(bias, dec).reshape(SAMPLE_ROWS, 1)
    pad_slots = ((0, 0), (0, 0), (0, PAGE_SIZE - dec))
    knt = jnp.pad(jnp.transpose(k_s.reshape(bs, dec, ATT_WIDTH), (0, 2, 1)), pad_slots)
    vnt = jnp.pad(jnp.transpose(v_s.reshape(bs, dec, ATT_WIDTH), (0, 2, 1)), pad_slots)
    o_p, ot = _attention(q_p, kb_p, vb_p, bias, _neg_suffix_matrix(K_TILE),
                         page_table, qbd, bcol, _neg_suffix_matrix(PAGE_SIZE), knt, vnt,
                         _slot_minor_pages(cache_k[l]), _slot_minor_pages(cache_v[l]))
    y_p = _merge(x_prompt, o_p, yc_p, sga_p, sgb_p, mp[2], mp[3], mp[4], mp[5], g2, wa, wc, wo, w1, w2,
                 tm=ROW_TILE)
    ot = ot[:, :, :SAMPLE_ROWS].reshape(bs, N_HEADS, HEAD_DIM, dec, N_HEADS)
    o_s = jnp.einsum('bhdth->bthd', ot).reshape(1, rows, ATT_WIDTH).astype(BF16)
    y_s = _merge(xs, o_s, yc_s, sga_s, sgb_s, ms_[2], ms_[3], ms_[4], ms_[5], g2, wa, wc, wo, w1, w2, tm=rows)

    heads = (N_HEADS, HEAD_DIM)
    return (y_p,
            y_s.reshape(bs, dec, d),
            k_p.reshape(1, bp, seq, *heads), v_p.reshape(1, bp, seq, *heads),
            cs_p.reshape(1, bp, CONV_WIDTH - 1, CONV_CH),
            k_s.reshape(1, bs, dec, *heads), v_s.reshape(1, bs, dec, *heads),
            u_s.reshape(bs, dec, CONV_CH)[:, dec - (CONV_WIDTH - 1):].reshape(1, bs, CONV_WIDTH - 1, CONV_CH))
```

```python
import functools
import math

import jax
import jax.numpy as jnp
from jax import lax
from jax.experimental import pallas as pl
from jax.experimental.pallas import tpu as pltpu

D_MODEL = 1024
N_HEADS = 8
HEAD_DIM = 64
ATT_WIDTH = N_HEADS * HEAD_DIM
CONV_CH = 512
CONV_WIDTH = 3
D_FF = 4 * D_MODEL
N_MOD = 6
PAGE_SIZE = 128
EPS = 1e-6
SB_SCALE = 1.0 / math.sqrt(HEAD_DIM)
LOG2E = math.log2(math.e)
Q_SCALE = SB_SCALE * LOG2E
IN_WIDTH = 3 * ATT_WIDTH + 3 * CONV_CH + 2 * D_MODEL

V7X_LANES = 128
V7X_SUBLANES = 8
V7X_MXU_DIM = 256
V7X_VMEM_BYTES = 64 * 1024 * 1024

ROW_TILE = 512
K_TILE = V7X_MXU_DIM
Q_TILE = K_TILE
PAGES_PER_STEP = 16
PAGE_BUFFERS = 2
HIST_ROWS = V7X_SUBLANES
SOFTPLUS2_LINEAR_FROM = 60.0
MASKED_LOG_WEIGHT = -1e30

F32 = jnp.float32
BF16 = jnp.bfloat16


def _vmem_limit(nbytes):
    return int(min(nbytes + (8 << 20), V7X_VMEM_BYTES - (6 << 20)))


def _resident(shape, index_map):
    return pl.BlockSpec(shape, index_map, pipeline_mode=pl.Buffered(1))


def _mm(lhs, rhs):
    return lax.dot_general(lhs, rhs, (((1,), (0,)), ((), ())), preferred_element_type=F32)


def _sigmoid(x):
    return 0.5 * jnp.tanh(0.5 * x) + 0.5


def _softplus2(z2):
    return jnp.log(1.0 + jnp.exp2(z2)) * LOG2E


def _ada_kernel(c_ref, w_ref, b_ref, o_ref):
    c = c_ref[...]
    s = (c * jax.nn.sigmoid(c)).astype(BF16)
    o_ref[...] = jnp.dot(s, w_ref[...], preferred_element_type=F32) + b_ref[...]


def _ada(c_all, w_ada, b_ada):
    n, d = c_all.shape
    width = w_ada.shape[1]
    tn = 1536
    return pl.pallas_call(
        _ada_kernel,
        grid=(width // tn,),
        in_specs=[pl.BlockSpec((n, d), lambda j: (0, 0)),
                  pl.BlockSpec((d, tn), lambda j: (0, j)),
                  pl.BlockSpec((1, tn), lambda j: (0, j))],
        out_specs=pl.BlockSpec((n, tn), lambda j: (0, j)),
        out_shape=jax.ShapeDtypeStruct((n, width), F32),
        compiler_params=pltpu.CompilerParams(
            dimension_semantics=("arbitrary",),
            vmem_limit_bytes=_vmem_limit(2 * d * tn * 4)),
        name="ada",
    )(c_all, w_ada, b_ada.reshape(1, width))


def _head_rms(p, gmat_ref, gain):
    sq = (p * p).astype(BF16)
    parts = []
    for c in range(ATT_WIDTH // V7X_MXU_DIM):
        sl = slice(c * V7X_MXU_DIM, (c + 1) * V7X_MXU_DIM)
        parts.append(jnp.dot(sq[:, sl], gmat_ref[...], preferred_element_type=F32))
    ms = jnp.concatenate(parts, axis=-1)
    return p * lax.rsqrt(ms + EPS) * gain


def _proj_kernel(*refs, tm, sample_mode):
    if sample_mode:
        (x_ref, sh_ref, sc_ref, g1_ref, w_ref, qg_ref, kg_ref, gmat_ref, cw_ref, cb_ref, h1_ref, h2_ref,
         q_out, k_out, kb_out, v_out, vb_out, yc_out, sga_out, sgb_out, u_out, conv_buf) = refs
    else:
        (x_ref, sh_ref, sc_ref, g1_ref, w_ref, qg_ref, kg_ref, gmat_ref, cw_ref, cb_ref,
         q_out, k_out, kb_out, v_out, vb_out, yc_out, sga_out, sgb_out, cs_out, conv_buf) = refs

        @pl.when((pl.program_id(0) == 0) & (pl.program_id(1) == 0))
        def _():
            conv_buf[0:HIST_ROWS, :] = jnp.zeros((HIST_ROWS, CONV_CH), F32)

    x = x_ref[0]
    ms = jnp.mean(x * x, axis=-1, keepdims=True)
    h = x * lax.rsqrt(ms + EPS) * g1_ref[...]
    h = (h * (1.0 + sc_ref[0]) + sh_ref[0]).astype(BF16)

    def part(lo, width):
        return jnp.dot(h, w_ref[:, lo:lo + width], preferred_element_type=F32)

    a = ATT_WIDTH
    c0 = 3 * a
    hc = part(c0, CONV_CH)
    bgate = part(c0 + CONV_CH, CONV_CH)
    cgate = part(c0 + 2 * CONV_CH, CONV_CH)
    u = cgate * hc

    g0 = c0 + 3 * CONV_CH
    sga_out[0] = _sigmoid(part(g0, D_MODEL)).astype(BF16)
    sgb_out[0] = _sigmoid(part(g0 + D_MODEL, D_MODEL)).astype(BF16)

    if sample_mode:
        u_out[0] = u
        prev = jnp.zeros((HIST_ROWS, CONV_CH), F32)
    else:
        first_tile = pl.program_id(1) == 0
        prev = jnp.where(first_tile, 0.0, conv_buf[0:HIST_ROWS, :])
    conv_buf[0:HIST_ROWS, :] = prev
    conv_buf[HIST_ROWS:HIST_ROWS + tm, :] = u
    u1 = conv_buf[HIST_ROWS - 1:HIST_ROWS - 1 + tm, :]
    u2 = conv_buf[HIST_ROWS - 2:HIST_ROWS - 2 + tm, :]
    if sample_mode:
        u1 = jnp.where(h1_ref[0] != 0, h1_ref[1], u1)
        u2 = jnp.where(h2_ref[0] != 0, h2_ref[1], u2)
    y = cb_ref[...] + cw_ref[0:1, :] * u2 + cw_ref[1:2, :] * u1 + cw_ref[2:3, :] * u
    yc_out[0] = (bgate * y).astype(BF16)
    if not sample_mode:
        conv_buf[0:HIST_ROWS, :] = u[tm - HIST_ROWS:, :]
        cs_out[0] = u[tm - (CONV_WIDTH - 1):, :]

    qn = _head_rms(part(0, a), gmat_ref, qg_ref[...])
    q_out[0] = (qn * Q_SCALE).astype(BF16)
    kn = _head_rms(part(a, a), gmat_ref, kg_ref[...])
    k_out[0] = kn.T
    kb_out[0] = kn.astype(BF16)
    v = part(2 * a, a)
    v_out[0] = v.T
    parity = (lax.broadcasted_iota(jnp.int32, v.shape, 1) // HEAD_DIM) % 2
    for p in range(2):
        vb_out[0, :, p * a:(p + 1) * a] = jnp.where(parity == p, v, 0.0).astype(BF16)


def _proj(x, sh, sc, g1, w_in_bf, qg, kg, gmat, conv_w, conv_b, hist=None):
    nb, t, d = x.shape
    sample_mode = hist is not None
    tm = t if sample_mode else ROW_TILE
    nt = t // tm
    mod_rows = sh.shape[1]
    mod_block = (1, tm, d) if mod_rows == t else (1, 1, d)
    mod_map = (lambda b, i: (b, i, 0)) if mod_rows == t else (lambda b, i: (b, 0, 0))
    row = lambda w: pl.BlockSpec((1, tm, w), lambda b, i: (b, i, 0))
    const = lambda shape: _resident(shape, lambda b, i: (0,) * len(shape))

    in_specs = [row(d), pl.BlockSpec(mod_block, mod_map), pl.BlockSpec(mod_block, mod_map),
                const((1, d)), const((d, IN_WIDTH)), const((1, ATT_WIDTH)), const((1, ATT_WIDTH)),
                const((V7X_MXU_DIM, V7X_MXU_DIM)), const((CONV_WIDTH, CONV_CH)), const((1, CONV_CH))]
    args = [x, sh, sc, g1, w_in_bf, qg, kg, gmat, conv_w, conv_b]
    out_shape = [jax.ShapeDtypeStruct((nb, t, ATT_WIDTH), BF16),
                 jax.ShapeDtypeStruct((nb, ATT_WIDTH, t), F32),
                 jax.ShapeDtypeStruct((nb, t, ATT_WIDTH), BF16),
                 jax.ShapeDtypeStruct((nb, ATT_WIDTH, t), F32),
                 jax.ShapeDtypeStruct((nb, t, 2 * ATT_WIDTH), BF16),
                 jax.ShapeDtypeStruct((nb, t, CONV_CH), BF16),
                 jax.ShapeDtypeStruct((nb, t, D_MODEL), BF16),
                 jax.ShapeDtypeStruct((nb, t, D_MODEL), BF16)]
    col = pl.BlockSpec((1, ATT_WIDTH, tm), lambda b, i: (b, 0, i))
    out_specs = [row(ATT_WIDTH), col, row(ATT_WIDTH), col,
                 row(2 * ATT_WIDTH), row(CONV_CH), row(D_MODEL), row(D_MODEL)]
    if sample_mode:
        in_specs += [const((2, t, CONV_CH)), const((2, t, CONV_CH))]
        args += list(hist)
        out_shape.append(jax.ShapeDtypeStruct((nb, t, CONV_CH), F32))
        out_specs.append(row(CONV_CH))
    else:
        out_shape.append(jax.ShapeDtypeStruct((nb, CONV_WIDTH - 1, CONV_CH), F32))
        out_specs.append(pl.BlockSpec((1, CONV_WIDTH - 1, CONV_CH), lambda b, i: (b, 0, 0)))

    est = (d * IN_WIDTH * 2 + 2 * tm * d * 4 + 2 * tm * (ATT_WIDTH * 16 + CONV_CH * 6 + D_MODEL * 4)
           + 6 * tm * D_MODEL * 4)
    return pl.pallas_call(
        functools.partial(_proj_kernel, tm=tm, sample_mode=sample_mode),
        grid=(nb, nt),
        in_specs=in_specs,
        out_specs=out_specs,
        out_shape=out_shape,
        scratch_shapes=[pltpu.VMEM((tm + HIST_ROWS, CONV_CH), F32)],
        compiler_params=pltpu.CompilerParams(
            dimension_semantics=("arbitrary", "arbitrary"),
            vmem_limit_bytes=_vmem_limit(est)),
        name="proj_sample" if sample_mode else "proj_prompt",
    )(*args)


def _prompt_attention_part(i, part, n_parts, bias_ref, q_ref, k_ref, v_ref, u_ref, o_ref,
                           qm_ref, z_buf, w_buf, carry_ref, acc_ref):
    tq = q_ref.shape[1]
    pair = V7X_LANES
    n_pairs = q_ref.shape[2] // pair
    per_pair = pair // HEAD_DIM

    def key_rows(j):
        return pl.ds(pl.multiple_of(j * K_TILE, K_TILE), K_TILE)

    pairs = range(n_pairs)

    def scores(j, slot, hps=pairs, hhs=range(per_pair)):
        for h in [hp * per_pair + hh for hp in hps for hh in hhs]:
            hp = h // per_pair
            kt = k_ref[0, key_rows(j), hp * pair:(hp + 1) * pair]
            z = lax.dot_general(qm_ref[h], kt, (((1,), (1,)), ((), ())), preferred_element_type=F32)
            z_buf[slot, h] = jnp.minimum(z + bias_ref[h], SOFTPLUS2_LINEAR_FROM)

    def log_weights(slot, diagonal, hps=pairs, hhs=range(per_pair)):
        for h in [hp * per_pair + hh for hp in hps for hh in hhs]:
            z = z_buf[slot, h]
            sp = _softplus2(z)
            if diagonal:
                row = lax.broadcasted_iota(jnp.int32, (tq, K_TILE), 0)
                col = lax.broadcasted_iota(jnp.int32, (tq, K_TILE), 1)
                valid = row > col
                sp = jnp.where(valid, sp, 0.0)
            p = _mm(sp, u_ref[...])
            carry = carry_ref[h]
            w = z + p + jnp.concatenate([carry] * (K_TILE // pair), axis=1)
            if diagonal:
                w = jnp.where(valid, w, MASKED_LOG_WEIGHT)
            w_buf[slot, h] = w
            carry_ref[h] = carry + jnp.broadcast_to(p[:, 0:1], carry.shape)

    def weighted_values(j, slot, hps=pairs):
        width = n_pairs * pair
        for hp in hps:
            out = None
            for hh in range(per_pair):
                vh = v_ref[0, key_rows(j), hh * width + hp * pair:hh * width + (hp + 1) * pair]
                contrib = _mm(jnp.exp2(w_buf[slot, hp * per_pair + hh]), vh)
                out = contrib if out is None else out + contrib
            acc_ref[hp] += out

    @pl.when(part == 0)
    def _():
        lane_q = lax.broadcasted_iota(jnp.int32, (tq, pair), 1)
        for hp in pairs:
            q2 = q_ref[0, :, hp * pair:(hp + 1) * pair]
            for hh in range(per_pair):
                qm_ref[hp * per_pair + hh] = jnp.where(lane_q // HEAD_DIM == hh, q2, jnp.zeros_like(q2))
        carry_ref[...] = jnp.zeros(carry_ref.shape, F32)
        acc_ref[...] = jnp.zeros(acc_ref.shape, F32)
        scores(i, 0)
        scores(jnp.maximum(i - 1, 0), 1)
        log_weights(0, diagonal=True)

    def step(n, slot):
        for hp in pairs:
            log_weights(1 - slot, False, [hp])
            scores(i - n - 2, slot, [hp])
            weighted_values(i - n, slot, [hp])

    def two_steps(m, _):
        step(2 * m, 0)
        step(2 * m + 1, 1)
        return 0

    n_full = jnp.maximum(i - 1, 0)
    n_loop = n_full // 2
    lax.fori_loop(n_loop * part // n_parts, n_loop * (part + 1) // n_parts, two_steps, 0)

    @pl.when(part == n_parts - 1)
    def _():
        @pl.when(n_full % 2 == 1)
        def _():
            step(n_full - 1, 0)

        @pl.when(i >= 1)
        def _():
            slot = (i - 1) % 2
            weighted_values(1, slot)
            log_weights(1 - slot, False)

        weighted_values(0, i % 2)
        for hp in pairs:
            o_ref[0, :, hp * pair:(hp + 1) * pair] = acc_ref[hp].astype(o_ref.dtype)


SAMPLE_ROWS = 32
QUAD = V7X_LANES // SAMPLE_ROWS


def _sample_attention_chunk(c, n_chunks, qbd_ref, bcol_ref, u_ref, knt_ref, vnt_ref, kt_refs, vt_refs,
                            o_ref, acc_ref, carry_ref):
    n_pages = len(kt_refs)
    rows = SAMPLE_ROWS
    qbd = qbd_ref[0]
    bcol = bcol_ref[...]
    col_block = lax.broadcasted_iota(jnp.int32, (PAGE_SIZE, V7X_LANES), 1) // rows

    def scores(kt):
        z2 = jnp.dot(qbd, kt, preferred_element_type=F32) + bcol
        return jnp.minimum(z2, SOFTPLUS2_LINEAR_FROM)

    def suffix(sp):
        return jnp.dot(sp.astype(BF16), u_ref[...], preferred_element_type=F32)

    def transposed_weights(a_blocks):
        at = jnp.concatenate(a_blocks, axis=0).T
        return [jnp.where(col_block == p, at, 0.0) for p in range(len(a_blocks))]

    @pl.when(c == 0)
    def _():
        z = scores(knt_ref[0])
        key = lax.broadcasted_iota(jnp.int32, (rows, PAGE_SIZE), 1)
        qry = lax.broadcasted_iota(jnp.int32, (rows, PAGE_SIZE), 0) // N_HEADS
        valid = key < qry
        sp = jnp.where(valid, _softplus2(z), 0.0)
        p = suffix(sp)
        a = jnp.where(valid, jnp.exp2(z + p), 0.0)
        zero = jnp.zeros_like(a)
        w = transposed_weights([a] + [zero] * (QUAD - 1))[0]
        acc_ref[...] = jnp.dot(vnt_ref[0], w, preferred_element_type=F32)
        carry_ref[...] = p[:, 0:1]

    zs = [scores(kt_refs[g][...]) for g in range(n_pages)]
    z_all = jnp.concatenate(zs, axis=0)
    p_all = suffix(_softplus2(z_all))
    carry = carry_ref[...]
    carries = [None] * n_pages
    for g in reversed(range(n_pages)):
        carries[g] = carry
        carry = carry + p_all[g * rows:(g + 1) * rows, 0:1]
    carry_ref[...] = carry
    a_all = jnp.exp2(z_all + p_all + jnp.concatenate(carries, axis=0))

    acc = acc_ref[...]
    for q0 in range(0, n_pages, QUAD):
        w = transposed_weights([a_all[(q0 + p) * rows:(q0 + p + 1) * rows, :] for p in range(QUAD)])
        for p in range(0, QUAD, 2):
            vt2 = jnp.concatenate([vt_refs[q0 + p][...], vt_refs[q0 + p + 1][...]], axis=1)
            w2 = jnp.concatenate([w[p], w[p + 1]], axis=0)
            acc = acc + jnp.dot(vt2, w2, preferred_element_type=F32)
    acc_ref[...] = acc

    @pl.when(c == n_chunks - 1)
    def _():
        total = acc
        for p in range(1, QUAD):
            total = total + pltpu.roll(acc, p * rows, axis=1)
        o_ref[0] = total


PROMPT_PARTS = 4


def _attention_kernel(order_ref, bias_ref, q_ref, k_ref, v_ref, up_ref, qbd_ref, bcol_ref, us_ref, knt_ref, vnt_ref,
                      ck_hbm, cv_hbm, o_ref, os_ref, qm_ref, z_buf, w_buf, carry_ref, acc_ref, s_acc_ref,
                      s_carry_ref, page_buf, page_sem, *, n_pages, n_chunks):
    i = pl.program_id(1)
    part = pl.program_id(2)
    n_steps = pl.num_programs(0) * pl.num_programs(1) * PROMPT_PARTS
    step = (pl.program_id(0) * pl.num_programs(1) + i) * PROMPT_PARTS + part
    half = step % 2

    def page_copies(s, into):
        copies = []
        for g in range(n_pages):
            page = order_ref[s, g]
            copies.append(pltpu.make_async_copy(ck_hbm.at[page], page_buf.at[into, 0, g], page_sem.at[into]))
            copies.append(pltpu.make_async_copy(cv_hbm.at[page], page_buf.at[into, 1, g], page_sem.at[into]))
        return copies

    @pl.when(step == 0)
    def _():
        for copy in page_copies(0, 0):
            copy.start()

    @pl.when(step + 1 < n_steps)
    def _():
        for copy in page_copies(step + 1, 1 - half):
            copy.start()

    for copy in page_copies(step, half):
        copy.wait()
    kt_refs = [page_buf.at[half, 0, g] for g in range(n_pages)]
    vt_refs = [page_buf.at[half, 1, g] for g in range(n_pages)]
    _sample_attention_chunk(step % n_chunks, n_chunks, qbd_ref, bcol_ref, us_ref, knt_ref, vnt_ref,
                            kt_refs, vt_refs, os_ref, s_acc_ref, s_carry_ref)
    _prompt_attention_part(i, part, PROMPT_PARTS, bias_ref, q_ref, k_ref, v_ref, up_ref, o_ref,
                           qm_ref, z_buf, w_buf, carry_ref, acc_ref)


def _attention(q_bf, k_bf, v_bf2, bias, u_prompt, page_table, qbd, bcol, u_sample, knt, vnt, cache_kt, cache_vt):
    nb, t, width = q_bf.shape
    bs, n_tab = page_table.shape
    tq = Q_TILE
    nq = t // tq
    pair = V7X_LANES
    g = PAGES_PER_STEP
    n_chunks = n_tab // g
    rows = SAMPLE_ROWS
    assert qbd.shape[1] == rows and g % QUAD == 0
    assert nb * nq * PROMPT_PARTS == bs * n_chunks, "one page chunk per grid step"
    page_block = (1, ATT_WIDTH, PAGE_SIZE)
    page_order = page_table.reshape(bs, n_chunks, g)[:, ::-1, :].reshape(bs * n_chunks, g)

    def seq_of(b, i, c):
        return ((b * nq + i) * PROMPT_PARTS + c) // n_chunks

    per_seq = lambda shape: pl.BlockSpec(shape, lambda b, i, c, po, bs_: (seq_of(b, i, c), 0, 0))
    const = lambda shape: pl.BlockSpec(shape, lambda b, i, c, po, bs_: (0,) * len(shape))
    in_specs = [pl.BlockSpec((1, tq, width), lambda b, i, c, po, bs_: (b, i, 0)),
                _resident((1, t, width), lambda b, i, c, po, bs_: (b, 0, 0)),
                _resident((1, t, 2 * width), lambda b, i, c, po, bs_: (b, 0, 0)),
                _resident((K_TILE, K_TILE), lambda b, i, c, po, bs_: (0, 0)),
                per_seq((1, rows, ATT_WIDTH)), const((rows, 1)), const((PAGE_SIZE, PAGE_SIZE)),
                per_seq(page_block), per_seq(page_block),
                pl.BlockSpec(memory_space=pl.ANY), pl.BlockSpec(memory_space=pl.ANY)]
    tile_bytes = tq * K_TILE * 4
    scratch = [pltpu.VMEM((N_HEADS, tq, pair), BF16),
               pltpu.VMEM((2, N_HEADS, tq, K_TILE), F32),
               pltpu.VMEM((2, N_HEADS, tq, K_TILE), F32),
               pltpu.VMEM((N_HEADS, tq, pair), F32),
               pltpu.VMEM((width // pair, tq, pair), F32),
               pltpu.VMEM((ATT_WIDTH, V7X_LANES), F32),
               pltpu.VMEM((rows, 1), F32),
               pltpu.VMEM((PAGE_BUFFERS, 2, g, ATT_WIDTH, PAGE_SIZE), F32),
               pltpu.SemaphoreType.DMA((PAGE_BUFFERS,))]
    grid_spec = pltpu.PrefetchScalarGridSpec(
        num_scalar_prefetch=2,
        grid=(nb, nq, PROMPT_PARTS),
        in_specs=in_specs,
        out_specs=[pl.BlockSpec((1, tq, width), lambda b, i, c, po, bs_: (b, i, 0)),
                   pl.BlockSpec((1, ATT_WIDTH, V7X_LANES), lambda b, i, c, po, bs_: (seq_of(b, i, c), 0, 0))],
        scratch_shapes=scratch)
    est = (3 * t * width * 2 + 4 * tq * width * 2 + 4 * N_HEADS * tile_bytes
           + N_HEADS * tq * pair * (2 + 4) + 4 * tq * pair * 4 + 8 * tile_bytes
           + PAGE_BUFFERS * 2 * g * PAGE_SIZE * ATT_WIDTH * 4 + 6 * PAGE_SIZE * ATT_WIDTH * 4)
    return pl.pallas_call(
        functools.partial(_attention_kernel, n_pages=g, n_chunks=n_chunks),
        grid_spec=grid_spec,
        out_shape=[jax.ShapeDtypeStruct((nb, t, width), BF16),
                   jax.ShapeDtypeStruct((bs, ATT_WIDTH, V7X_LANES), F32)],
        compiler_params=pltpu.CompilerParams(
            dimension_semantics=("arbitrary", "arbitrary", "arbitrary"),
            vmem_limit_bytes=_vmem_limit(est)),
        name="attention",
    )(page_order, bias, q_bf, k_bf, v_bf2, u_prompt, qbd, bcol, u_sample, knt, vnt, cache_kt, cache_vt)


def _merge_kernel(x_ref, o_ref, yc_ref, sga_ref, sgb_ref, gt1_ref, sh2_ref, sc2_ref, gt2_ref, g2_ref,
                  wa_ref, wc_ref, wo_ref, w1_ref, w2_ref, out_ref):
    ya = jnp.dot(o_ref[0], wa_ref[...], preferred_element_type=F32)
    yb = jnp.dot(yc_ref[0], wc_ref[...], preferred_element_type=F32)
    mixed = (sga_ref[0].astype(F32) * ya + sgb_ref[0].astype(F32) * yb).astype(BF16)
    x1 = x_ref[0] + gt1_ref[0] * jnp.dot(mixed, wo_ref[...], preferred_element_type=F32)
    ms = jnp.mean(x1 * x1, axis=-1, keepdims=True)
    h2 = x1 * lax.rsqrt(ms + EPS) * g2_ref[...]
    h2 = (h2 * (1.0 + sc2_ref[0]) + sh2_ref[0]).astype(BF16)
    chunk = D_MODEL
    mlp = jnp.zeros(x1.shape, F32)
    for c in range(D_FF // chunk):
        hid = jnp.maximum(jnp.dot(h2, w1_ref[:, c * chunk:(c + 1) * chunk], preferred_element_type=F32), 0.0)
        hid = (hid * hid).astype(BF16)
        mlp = mlp + jnp.dot(hid, w2_ref[c * chunk:(c + 1) * chunk, :], preferred_element_type=F32)
    out_ref[0] = x1 + gt2_ref[0] * mlp


def _merge(x, o_att, yc, sga, sgb, gt1, sh2, sc2, gt2, g2, wa, wc, wo, w1, w2, tm):
    nb, t, d = x.shape
    nt = t // tm
    mod_rows = gt1.shape[1]
    mod_block = (1, tm, d) if mod_rows == t else (1, 1, d)
    mod_map = (lambda b, i: (b, i, 0)) if mod_rows == t else (lambda b, i: (b, 0, 0))
    row = lambda w: pl.BlockSpec((1, tm, w), lambda b, i: (b, i, 0))
    mod = pl.BlockSpec(mod_block, mod_map)
    const = lambda shape: _resident(shape, lambda b, i: (0,) * len(shape))
    w_bytes = 2 * (2 * ATT_WIDTH * d + d * d + 2 * d * D_FF)
    est = w_bytes + 2 * tm * (4 * d + 2 * ATT_WIDTH * 2 + 2 * d * 2 + 4 * d) + 8 * tm * d * 4
    return pl.pallas_call(
        _merge_kernel,
        grid=(nb, nt),
        in_specs=[row(d), row(ATT_WIDTH), row(CONV_CH), row(d), row(d), mod, mod, mod, mod, const((1, d)),
                  const((ATT_WIDTH, d)), const((CONV_CH, d)), const((d, d)), const((d, D_FF)), const((D_FF, d))],
        out_specs=row(d),
        out_shape=jax.ShapeDtypeStruct((nb, t, d), F32),
        compiler_params=pltpu.CompilerParams(
            dimension_semantics=("arbitrary", "arbitrary"),
            vmem_limit_bytes=_vmem_limit(est)),
        name="merge_mlp",
    )(x, o_att, yc, sga, sgb, gt1, sh2, sc2, gt2, g2, wa, wc, wo, w1, w2)


def _neg_suffix_matrix(n):
    j = jnp.arange(n)[:, None]
    s = jnp.arange(n)[None, :]
    return jnp.where(j >= s, -1.0, 0.0).astype(BF16)


def _slot_minor_pages(cache):
    n_phys, slots, heads, dim = cache.shape
    return jnp.transpose(cache, (0, 2, 3, 1)).reshape(n_phys, heads * dim, slots)


def kernel(x_prompt, x_sample, c_prompt, c_sample, cache_k, cache_v, state_conv, page_table, rms_g1, rms_g2,
           w_ada, b_ada, w_in, q_norm_g, k_norm_g, sb_bias, conv_w, conv_b, w_att_out, w_conv_out, w_o, w_mlp1,
           w_mlp2):
    depth = w_in.shape[0]
    assert depth == 1, "single-layer step"
    l = 0
    bp, seq, d = x_prompt.shape
    bs, dec, _ = x_sample.shape
    assert dec * N_HEADS == SAMPLE_ROWS

    w_in_bf = w_in[l].astype(BF16)
    wa, wc, wo = w_att_out[l].astype(BF16), w_conv_out[l].astype(BF16), w_o[l].astype(BF16)
    w1, w2 = w_mlp1[l].astype(BF16), w_mlp2[l].astype(BF16)
    g1 = rms_g1[l].reshape(1, d)
    g2 = rms_g2[l].reshape(1, d)
    qg = jnp.tile(q_norm_g[l], N_HEADS).reshape(1, ATT_WIDTH)
    kg = jnp.tile(k_norm_g[l], N_HEADS).reshape(1, ATT_WIDTH)
    cw = conv_w[l]
    cb = conv_b[l].reshape(1, CONV_CH)
    grp = jnp.arange(V7X_MXU_DIM) // HEAD_DIM
    gmat = jnp.where(grp[:, None] == grp[None, :], 1.0 / HEAD_DIM, 0.0).astype(BF16)
    bias = sb_bias[l].astype(F32) * LOG2E

    mod = _ada(jnp.concatenate([c_prompt, c_sample], axis=0), w_ada[l], b_ada[l])
    mods = [mod[:, j * d:(j + 1) * d] for j in range(N_MOD)]
    mp = [m[:bp].reshape(bp, 1, d) for m in mods]
    ms_ = [jnp.repeat(m[bp:], dec, axis=0).reshape(1, bs * dec, d) for m in mods]

    (q_p, k_p, kb_p, v_p, vb_p, yc_p, sga_p, sgb_p, cs_p) = _proj(
        x_prompt, mp[0], mp[1], g1, w_in_bf, qg, kg, gmat, cw, cb)

    rows = bs * dec
    xs = x_sample.reshape(1, rows, d)
    st = state_conv[l]
    step = jnp.tile(jnp.arange(dec), bs)[:, None]
    flag1 = jnp.broadcast_to(step < 1, (rows, CONV_CH)).astype(F32)
    flag2 = jnp.broadcast_to(step < 2, (rows, CONV_CH)).astype(F32)
    zero = jnp.zeros((bs, dec - 2, CONV_CH), F32)
    val1 = jnp.concatenate([st[:, 1:2], zero, zero[:, :1]], axis=1).reshape(rows, CONV_CH)
    val2 = jnp.concatenate([st, zero], axis=1).reshape(rows, CONV_CH)
    hist = (jnp.stack([flag1, val1]), jnp.stack([flag2, val2]))
    (q_s, k_s, _, v_s, _, yc_s, sga_s, sgb_s, u_s) = _proj(
        xs, ms_[0], ms_[1], g1, w_in_bf, qg, kg, gmat, cw, cb, hist=hist)

    q4 = q_s.reshape(bs, dec, N_HEADS, HEAD_DIM)
    eye = jnp.eye(N_HEADS, dtype=BF16)
    qbd = jnp.einsum('bthd,hg->bthgd', q4, eye).reshape(bs, SAMPLE_ROWS, ATT_WIDTH)
    bcol = jnp.tile(bias, dec).reshape(SAMPLE_ROWS, 1)
    pad_slots = ((0, 0), (0, 0), (0, PAGE_SIZE - dec))
    knt = jnp.pad(jnp.transpose(k_s.reshape(ATT_WIDTH, bs, dec), (1, 0, 2)), pad_slots)
    vnt = jnp.pad(jnp.transpose(v_s.reshape(ATT_WIDTH, bs, dec), (1, 0, 2)), pad_slots)
    o_p, ot = _attention(q_p, kb_p, vb_p, bias, _neg_suffix_matrix(K_TILE),
                         page_table, qbd, bcol, _neg_suffix_matrix(PAGE_SIZE), knt, vnt,
                         _slot_minor_pages(cache_k[l]), _slot_minor_pages(cache_v[l]))
    y_p = _merge(x_prompt, o_p, yc_p, sga_p, sgb_p, mp[2], mp[3], mp[4], mp[5], g2, wa, wc, wo, w1, w2,
                 tm=ROW_TILE)
    ot = ot[:, :, :SAMPLE_ROWS].reshape(bs, N_HEADS, HEAD_DIM, dec, N_HEADS)
    o_s = jnp.einsum('bhdth->bthd', ot).reshape(1, rows, ATT_WIDTH).astype(BF16)
    y_s = _merge(xs, o_s, yc_s, sga_s, sgb_s, ms_[2], ms_[3], ms_[4], ms_[5], g2, wa, wc, wo, w1, w2, tm=rows)

    def seq_major(xt, nb_, t_):
        x4 = xt.reshape(xt.shape[0], N_HEADS, HEAD_DIM, -1)
        return jnp.transpose(x4, (0, 3, 1, 2)).reshape(1, nb_, t_, N_HEADS, HEAD_DIM)

    return (y_p,
            y_s.reshape(bs, dec, d),
            seq_major(k_p, bp, seq), seq_major(v_p, bp, seq),
            cs_p.reshape(1, bp, CONV_WIDTH - 1, CONV_CH),
            seq_major(k_s, bs, dec), seq_major(v_s, bs, dec),
            u_s.reshape(bs, dec, CONV_CH)[:, dec - (CONV_WIDTH - 1):].reshape(1, bs, CONV_WIDTH - 1, CONV_CH))
```

```python
import functools
import math

import jax
import jax.numpy as jnp
from jax import lax
from jax.experimental import pallas as pl
from jax.experimental.pallas import tpu as pltpu

D_MODEL = 1024
N_HEADS = 8
HEAD_DIM = 64
ATT_WIDTH = N_HEADS * HEAD_DIM
CONV_CH = 512
CONV_WIDTH = 3
D_FF = 4 * D_MODEL
N_MOD = 6
PAGE_SIZE = 128
EPS = 1e-6
SB_SCALE = 1.0 / math.sqrt(HEAD_DIM)
LOG2E = math.log2(math.e)
Q_SCALE = SB_SCALE * LOG2E
IN_WIDTH = 3 * ATT_WIDTH + 3 * CONV_CH + 2 * D_MODEL

V7X_LANES = 128
V7X_SUBLANES = 8
V7X_MXU_DIM = 256
V7X_VMEM_BYTES = 64 * 1024 * 1024

ROW_TILE = 512
K_TILE = V7X_MXU_DIM
Q_TILE = K_TILE
PAGES_PER_STEP = 16
PAGE_BUFFERS = 2
HIST_ROWS = V7X_SUBLANES
SOFTPLUS2_LINEAR_FROM = 60.0
MASKED_LOG_WEIGHT = -1e30

F32 = jnp.float32
BF16 = jnp.bfloat16


def _vmem_limit(nbytes):
    return int(min(nbytes + (8 << 20), V7X_VMEM_BYTES - (6 << 20)))


def _resident(shape, index_map):
    return pl.BlockSpec(shape, index_map, pipeline_mode=pl.Buffered(1))


def _mm(lhs, rhs):
    return lax.dot_general(lhs, rhs, (((1,), (0,)), ((), ())), preferred_element_type=F32)


def _sigmoid(x):
    return 0.5 * jnp.tanh(0.5 * x) + 0.5


def _softplus2(z2):
    return jnp.log(1.0 + jnp.exp2(z2)) * LOG2E


def _ada_kernel(c_ref, w_ref, b_ref, o_ref):
    c = c_ref[...]
    s = (c * jax.nn.sigmoid(c)).astype(BF16)
    o_ref[...] = jnp.dot(s, w_ref[...], preferred_element_type=F32) + b_ref[...]


def _ada(c_all, w_ada, b_ada):
    n, d = c_all.shape
    width = w_ada.shape[1]
    tn = 1536
    return pl.pallas_call(
        _ada_kernel,
        grid=(width // tn,),
        in_specs=[pl.BlockSpec((n, d), lambda j: (0, 0)),
                  pl.BlockSpec((d, tn), lambda j: (0, j)),
                  pl.BlockSpec((1, tn), lambda j: (0, j))],
        out_specs=pl.BlockSpec((n, tn), lambda j: (0, j)),
        out_shape=jax.ShapeDtypeStruct((n, width), F32),
        compiler_params=pltpu.CompilerParams(
            dimension_semantics=("arbitrary",),
            vmem_limit_bytes=_vmem_limit(2 * d * tn * 4)),
        name="ada",
    )(c_all, w_ada, b_ada.reshape(1, width))


def _head_rms(p, gmat_ref, gain):
    sq = (p * p).astype(BF16)
    parts = []
    for c in range(ATT_WIDTH // V7X_MXU_DIM):
        sl = slice(c * V7X_MXU_DIM, (c + 1) * V7X_MXU_DIM)
        parts.append(jnp.dot(sq[:, sl], gmat_ref[...], preferred_element_type=F32))
    ms = jnp.concatenate(parts, axis=-1)
    return p * lax.rsqrt(ms + EPS) * gain


def _proj_kernel(*refs, tm, sample_mode):
    if sample_mode:
        (x_ref, sh_ref, sc_ref, g1_ref, w_ref, qg_ref, kg_ref, gmat_ref, cw_ref, cb_ref, h1_ref, h2_ref,
         q_out, k_out, kb_out, v_out, vb_out, yc_out, sga_out, sgb_out, u_out, conv_buf) = refs
    else:
        (x_ref, sh_ref, sc_ref, g1_ref, w_ref, qg_ref, kg_ref, gmat_ref, cw_ref, cb_ref,
         q_out, k_out, kb_out, v_out, vb_out, yc_out, sga_out, sgb_out, cs_out, conv_buf) = refs

        @pl.when((pl.program_id(0) == 0) & (pl.program_id(1) == 0))
        def _():
            conv_buf[0:HIST_ROWS, :] = jnp.zeros((HIST_ROWS, CONV_CH), F32)

    x = x_ref[0]
    ms = jnp.mean(x * x, axis=-1, keepdims=True)
    h = x * lax.rsqrt(ms + EPS) * g1_ref[...]
    h = (h * (1.0 + sc_ref[0]) + sh_ref[0]).astype(BF16)

    def part(lo, width):
        return jnp.dot(h, w_ref[:, lo:lo + width], preferred_element_type=F32)

    a = ATT_WIDTH
    c0 = 3 * a
    hc = part(c0, CONV_CH)
    bgate = part(c0 + CONV_CH, CONV_CH)
    cgate = part(c0 + 2 * CONV_CH, CONV_CH)
    u = cgate * hc

    g0 = c0 + 3 * CONV_CH
    sga_out[0] = _sigmoid(part(g0, D_MODEL)).astype(BF16)
    sgb_out[0] = _sigmoid(part(g0 + D_MODEL, D_MODEL)).astype(BF16)

    if sample_mode:
        u_out[0] = u
        prev = jnp.zeros((HIST_ROWS, CONV_CH), F32)
    else:
        first_tile = pl.program_id(1) == 0
        prev = jnp.where(first_tile, 0.0, conv_buf[0:HIST_ROWS, :])
    conv_buf[0:HIST_ROWS, :] = prev
    conv_buf[HIST_ROWS:HIST_ROWS + tm, :] = u
    u1 = conv_buf[HIST_ROWS - 1:HIST_ROWS - 1 + tm, :]
    u2 = conv_buf[HIST_ROWS - 2:HIST_ROWS - 2 + tm, :]
    if sample_mode:
        u1 = jnp.where(h1_ref[0] != 0, h1_ref[1], u1)
        u2 = jnp.where(h2_ref[0] != 0, h2_ref[1], u2)
    y = cb_ref[...] + cw_ref[0:1, :] * u2 + cw_ref[1:2, :] * u1 + cw_ref[2:3, :] * u
    yc_out[0] = (bgate * y).astype(BF16)
    if not sample_mode:
        conv_buf[0:HIST_ROWS, :] = u[tm - HIST_ROWS:, :]
        cs_out[0] = u[tm - (CONV_WIDTH - 1):, :]

    qn = _head_rms(part(0, a), gmat_ref, qg_ref[...])
    q_out[0] = (qn * Q_SCALE).astype(BF16)
    kn = _head_rms(part(a, a), gmat_ref, kg_ref[...])
    k_out[0] = kn.T
    kb_out[0] = kn.astype(BF16)
    v = part(2 * a, a)
    v_out[0] = v.T
    parity = (lax.broadcasted_iota(jnp.int32, v.shape, 1) // HEAD_DIM) % 2
    for p in range(2):
        vb_out[0, :, p * a:(p + 1) * a] = jnp.where(parity == p, v, 0.0).astype(BF16)


def _proj(x, sh, sc, g1, w_in_bf, qg, kg, gmat, conv_w, conv_b, hist=None):
    nb, t, d = x.shape
    sample_mode = hist is not None
    tm = t if sample_mode else ROW_TILE
    nt = t // tm
    mod_rows = sh.shape[1]
    mod_block = (1, tm, d) if mod_rows == t else (1, 1, d)
    mod_map = (lambda b, i: (b, i, 0)) if mod_rows == t else (lambda b, i: (b, 0, 0))
    row = lambda w: pl.BlockSpec((1, tm, w), lambda b, i: (b, i, 0))
    const = lambda shape: _resident(shape, lambda b, i: (0,) * len(shape))

    in_specs = [row(d), pl.BlockSpec(mod_block, mod_map), pl.BlockSpec(mod_block, mod_map),
                const((1, d)), const((d, IN_WIDTH)), const((1, ATT_WIDTH)), const((1, ATT_WIDTH)),
                const((V7X_MXU_DIM, V7X_MXU_DIM)), const((CONV_WIDTH, CONV_CH)), const((1, CONV_CH))]
    args = [x, sh, sc, g1, w_in_bf, qg, kg, gmat, conv_w, conv_b]
    out_shape = [jax.ShapeDtypeStruct((nb, t, ATT_WIDTH), BF16),
                 jax.ShapeDtypeStruct((nb, ATT_WIDTH, t), F32),
                 jax.ShapeDtypeStruct((nb, t, ATT_WIDTH), BF16),
                 jax.ShapeDtypeStruct((nb, ATT_WIDTH, t), F32),
                 jax.ShapeDtypeStruct((nb, t, 2 * ATT_WIDTH), BF16),
                 jax.ShapeDtypeStruct((nb, t, CONV_CH), BF16),
                 jax.ShapeDtypeStruct((nb, t, D_MODEL), BF16),
                 jax.ShapeDtypeStruct((nb, t, D_MODEL), BF16)]
    col = pl.BlockSpec((1, ATT_WIDTH, tm), lambda b, i: (b, 0, i))
    out_specs = [row(ATT_WIDTH), col, row(ATT_WIDTH), col,
                 row(2 * ATT_WIDTH), row(CONV_CH), row(D_MODEL), row(D_MODEL)]
    if sample_mode:
        in_specs += [const((2, t, CONV_CH)), const((2, t, CONV_CH))]
        args += list(hist)
        out_shape.append(jax.ShapeDtypeStruct((nb, t, CONV_CH), F32))
        out_specs.append(row(CONV_CH))
    else:
        out_shape.append(jax.ShapeDtypeStruct((nb, CONV_WIDTH - 1, CONV_CH), F32))
        out_specs.append(pl.BlockSpec((1, CONV_WIDTH - 1, CONV_CH), lambda b, i: (b, 0, 0)))

    est = (d * IN_WIDTH * 2 + 2 * tm * d * 4 + 2 * tm * (ATT_WIDTH * 16 + CONV_CH * 6 + D_MODEL * 4)
           + 6 * tm * D_MODEL * 4)
    return pl.pallas_call(
        functools.partial(_proj_kernel, tm=tm, sample_mode=sample_mode),
        grid=(nb, nt),
        in_specs=in_specs,
        out_specs=out_specs,
        out_shape=out_shape,
        scratch_shapes=[pltpu.VMEM((tm + HIST_ROWS, CONV_CH), F32)],
        compiler_params=pltpu.CompilerParams(
            dimension_semantics=("arbitrary", "arbitrary"),
            vmem_limit_bytes=_vmem_limit(est)),
        name="proj_sample" if sample_mode else "proj_prompt",
    )(*args)


def _prompt_attention_part(i, part, n_parts, bias_ref, q_ref, k_ref, v_ref, u_ref, o_ref,
                           qm_ref, z_buf, w_buf, carry_ref, acc_ref, side_work):
    tq = q_ref.shape[1]
    pair = V7X_LANES
    n_pairs = q_ref.shape[2] // pair
    per_pair = pair // HEAD_DIM

    def key_rows(j):
        return pl.ds(pl.multiple_of(j * K_TILE, K_TILE), K_TILE)

    pairs = range(n_pairs)

    def scores(j, slot, hps=pairs, hhs=range(per_pair)):
        for h in [hp * per_pair + hh for hp in hps for hh in hhs]:
            hp = h // per_pair
            kt = k_ref[0, key_rows(j), hp * pair:(hp + 1) * pair]
            z = lax.dot_general(qm_ref[h], kt, (((1,), (1,)), ((), ())), preferred_element_type=F32)
            z_buf[slot, h] = jnp.minimum(z + bias_ref[h], SOFTPLUS2_LINEAR_FROM)

    def log_weights(slot, diagonal, hps=pairs, hhs=range(per_pair)):
        for h in [hp * per_pair + hh for hp in hps for hh in hhs]:
            z = z_buf[slot, h]
            sp = _softplus2(z)
            if diagonal:
                row = lax.broadcasted_iota(jnp.int32, (tq, K_TILE), 0)
                col = lax.broadcasted_iota(jnp.int32, (tq, K_TILE), 1)
                valid = row > col
                sp = jnp.where(valid, sp, 0.0)
            p = _mm(sp, u_ref[...])
            carry = carry_ref[h]
            w = z + p + jnp.concatenate([carry] * (K_TILE // pair), axis=1)
            if diagonal:
                w = jnp.where(valid, w, MASKED_LOG_WEIGHT)
            w_buf[slot, h] = w
            carry_ref[h] = carry + jnp.broadcast_to(p[:, 0:1], carry.shape)

    def weighted_values(j, slot, hps=pairs):
        width = n_pairs * pair
        for hp in hps:
            out = None
            for hh in range(per_pair):
                vh = v_ref[0, key_rows(j), hh * width + hp * pair:hh * width + (hp + 1) * pair]
                contrib = _mm(jnp.exp2(w_buf[slot, hp * per_pair + hh]), vh)
                out = contrib if out is None else out + contrib
            acc_ref[hp] += out

    @pl.when((part > 0) & (part < n_parts - 1))
    def _():
        side_work()

    @pl.when(part == 0)
    def _():
        lane_q = lax.broadcasted_iota(jnp.int32, (tq, pair), 1)
        for hp in pairs:
            q2 = q_ref[0, :, hp * pair:(hp + 1) * pair]
            for hh in range(per_pair):
                qm_ref[hp * per_pair + hh] = jnp.where(lane_q // HEAD_DIM == hh, q2, jnp.zeros_like(q2))
        carry_ref[...] = jnp.zeros(carry_ref.shape, F32)
        acc_ref[...] = jnp.zeros(acc_ref.shape, F32)
        scores(i, 0)
        scores(jnp.maximum(i - 1, 0), 1)
        side_work()
        log_weights(0, diagonal=True)

    def step(n, slot):
        for hp in pairs:
            log_weights(1 - slot, False, [hp])
            scores(i - n - 2, slot, [hp])
            weighted_values(i - n, slot, [hp])

    def two_steps(m, _):
        step(2 * m, 0)
        step(2 * m + 1, 1)
        return 0

    n_full = jnp.maximum(i - 1, 0)
    n_loop = n_full // 2
    lax.fori_loop(n_loop * part // n_parts, n_loop * (part + 1) // n_parts, two_steps, 0)

    @pl.when(part == n_parts - 1)
    def _():
        @pl.when(n_full % 2 == 1)
        def _():
            step(n_full - 1, 0)

        @pl.when(i == 0)
        def _():
            side_work()

        @pl.when(i >= 1)
        def _():
            side_work()
            slot = (i - 1) % 2
            weighted_values(1, slot)
            log_weights(1 - slot, False)

        weighted_values(0, i % 2)
        for hp in pairs:
            o_ref[0, :, hp * pair:(hp + 1) * pair] = acc_ref[hp].astype(o_ref.dtype)


SAMPLE_ROWS = 32
QUAD = V7X_LANES // SAMPLE_ROWS


def _sample_attention_chunk(qbd_ref, bcol_ref, u_ref, knt_ref, vnt_ref, kt_refs, vt_refs, o_ref, acc_ref, carry_ref):
    n_pages = len(kt_refs)
    rows = SAMPLE_ROWS
    qbd = qbd_ref[0]
    bcol = bcol_ref[...]
    col_block = lax.broadcasted_iota(jnp.int32, (PAGE_SIZE, V7X_LANES), 1) // rows

    def scores(kt):
        z2 = jnp.dot(qbd, kt, preferred_element_type=F32) + bcol
        return jnp.minimum(z2, SOFTPLUS2_LINEAR_FROM)

    def suffix(sp):
        return jnp.dot(sp.astype(BF16), u_ref[...], preferred_element_type=F32)

    def transposed_weights(a_blocks):
        at = jnp.concatenate(a_blocks, axis=0).T
        return [jnp.where(col_block == p, at, 0.0) for p in range(len(a_blocks))]

    def new_keys():
        z = scores(knt_ref[0])
        key = lax.broadcasted_iota(jnp.int32, (rows, PAGE_SIZE), 1)
        qry = lax.broadcasted_iota(jnp.int32, (rows, PAGE_SIZE), 0) // N_HEADS
        valid = key < qry
        sp = jnp.where(valid, _softplus2(z), 0.0)
        p = suffix(sp)
        a = jnp.where(valid, jnp.exp2(z + p), 0.0)
        zero = jnp.zeros_like(a)
        w = transposed_weights([a] + [zero] * (QUAD - 1))[0]
        acc_ref[...] = jnp.dot(vnt_ref[0], w, preferred_element_type=F32)
        carry_ref[...] = p[:, 0:1]

    def cached_pages():
        zs = [scores(kt_refs[g][...]) for g in range(n_pages)]
        z_all = jnp.concatenate(zs, axis=0)
        p_all = suffix(_softplus2(z_all))
        carry = carry_ref[...]
        carries = [None] * n_pages
        for g in reversed(range(n_pages)):
            carries[g] = carry
            carry = carry + p_all[g * rows:(g + 1) * rows, 0:1]
        carry_ref[...] = carry
        a_all = jnp.exp2(z_all + p_all + jnp.concatenate(carries, axis=0))

        acc = acc_ref[...]
        for q0 in range(0, n_pages, QUAD):
            w = transposed_weights([a_all[(q0 + p) * rows:(q0 + p + 1) * rows, :] for p in range(QUAD)])
            for p in range(0, QUAD, 2):
                vt2 = jnp.concatenate([vt_refs[q0 + p][...], vt_refs[q0 + p + 1][...]], axis=1)
                w2 = jnp.concatenate([w[p], w[p + 1]], axis=0)
                acc = acc + jnp.dot(vt2, w2, preferred_element_type=F32)
        acc_ref[...] = acc

    def finish():
        acc = acc_ref[...]
        total = acc
        for p in range(1, QUAD):
            total = total + pltpu.roll(acc, p * rows, axis=1)
        o_ref[0] = total

    return new_keys, cached_pages, finish


PROMPT_PARTS = 4


def _attention_kernel(order_ref, bias_ref, q_ref, k_ref, v_ref, up_ref, qbd_ref, bcol_ref, us_ref, knt_ref, vnt_ref,
                      ck_hbm, cv_hbm, o_ref, os_ref, qm_ref, z_buf, w_buf, carry_ref, acc_ref, s_acc_ref,
                      s_carry_ref, page_buf, page_sem, *, n_pages, n_chunks):
    i = pl.program_id(1)
    part = pl.program_id(2)
    n_steps = pl.num_programs(0) * pl.num_programs(1) * PROMPT_PARTS
    step = (pl.program_id(0) * pl.num_programs(1) + i) * PROMPT_PARTS + part
    half = step % 2

    def page_copies(s, into):
        copies = []
        for g in range(n_pages):
            page = order_ref[s, g]
            copies.append(pltpu.make_async_copy(ck_hbm.at[page], page_buf.at[into, 0, g], page_sem.at[into]))
            copies.append(pltpu.make_async_copy(cv_hbm.at[page], page_buf.at[into, 1, g], page_sem.at[into]))
        return copies

    @pl.when(step == 0)
    def _():
        for copy in page_copies(0, 0):
            copy.start()

    @pl.when(step + 1 < n_steps)
    def _():
        for copy in page_copies(step + 1, 1 - half):
            copy.start()

    for copy in page_copies(step, half):
        copy.wait()
    kt_refs = [page_buf.at[half, 0, g] for g in range(n_pages)]
    vt_refs = [page_buf.at[half, 1, g] for g in range(n_pages)]
    new_keys, cached_pages, finish = _sample_attention_chunk(
        qbd_ref, bcol_ref, us_ref, knt_ref, vnt_ref, kt_refs, vt_refs, os_ref, s_acc_ref, s_carry_ref)
    chunk = step % n_chunks
    pl.when(chunk == 0)(new_keys)
    _prompt_attention_part(i, part, PROMPT_PARTS, bias_ref, q_ref, k_ref, v_ref, up_ref, o_ref,
                           qm_ref, z_buf, w_buf, carry_ref, acc_ref, cached_pages)
    pl.when(chunk == n_chunks - 1)(finish)


def _attention(q_bf, k_bf, v_bf2, bias, u_prompt, page_table, qbd, bcol, u_sample, knt, vnt, cache_kt, cache_vt):
    nb, t, width = q_bf.shape
    bs, n_tab = page_table.shape
    tq = Q_TILE
    nq = t // tq
    pair = V7X_LANES
    g = PAGES_PER_STEP
    n_chunks = n_tab // g
    rows = SAMPLE_ROWS
    assert qbd.shape[1] == rows and g % QUAD == 0
    assert nb * nq * PROMPT_PARTS == bs * n_chunks, "one page chunk per grid step"
    page_block = (1, ATT_WIDTH, PAGE_SIZE)
    page_order = page_table.reshape(bs, n_chunks, g)[:, ::-1, :].reshape(bs * n_chunks, g)

    def seq_of(b, i, c):
        return ((b * nq + i) * PROMPT_PARTS + c) // n_chunks

    per_seq = lambda shape: pl.BlockSpec(shape, lambda b, i, c, po, bs_: (seq_of(b, i, c), 0, 0))
    const = lambda shape: pl.BlockSpec(shape, lambda b, i, c, po, bs_: (0,) * len(shape))
    in_specs = [pl.BlockSpec((1, tq, width), lambda b, i, c, po, bs_: (b, i, 0)),
                _resident((1, t, width), lambda b, i, c, po, bs_: (b, 0, 0)),
                _resident((1, t, 2 * width), lambda b, i, c, po, bs_: (b, 0, 0)),
                _resident((K_TILE, K_TILE), lambda b, i, c, po, bs_: (0, 0)),
                per_seq((1, rows, ATT_WIDTH)), const((rows, 1)), const((PAGE_SIZE, PAGE_SIZE)),
                per_seq(page_block), per_seq(page_block),
                pl.BlockSpec(memory_space=pl.ANY), pl.BlockSpec(memory_space=pl.ANY)]
    tile_bytes = tq * K_TILE * 4
    scratch = [pltpu.VMEM((N_HEADS, tq, pair), BF16),
               pltpu.VMEM((2, N_HEADS, tq, K_TILE), F32),
               pltpu.VMEM((2, N_HEADS, tq, K_TILE), F32),
               pltpu.VMEM((N_HEADS, tq, pair), F32),
               pltpu.VMEM((width // pair, tq, pair), F32),
               pltpu.VMEM((ATT_WIDTH, V7X_LANES), F32),
               pltpu.VMEM((rows, 1), F32),
               pltpu.VMEM((PAGE_BUFFERS, 2, g, ATT_WIDTH, PAGE_SIZE), F32),
               pltpu.SemaphoreType.DMA((PAGE_BUFFERS,))]
    grid_spec = pltpu.PrefetchScalarGridSpec(
        num_scalar_prefetch=2,
        grid=(nb, nq, PROMPT_PARTS),
        in_specs=in_specs,
        out_specs=[pl.BlockSpec((1, tq, width), lambda b, i, c, po, bs_: (b, i, 0)),
                   pl.BlockSpec((1, ATT_WIDTH, V7X_LANES), lambda b, i, c, po, bs_: (seq_of(b, i, c), 0, 0))],
        scratch_shapes=scratch)
    est = (3 * t * width * 2 + 4 * tq * width * 2 + 4 * N_HEADS * tile_bytes
           + N_HEADS * tq * pair * (2 + 4) + 4 * tq * pair * 4 + 8 * tile_bytes
           + PAGE_BUFFERS * 2 * g * PAGE_SIZE * ATT_WIDTH * 4 + 6 * PAGE_SIZE * ATT_WIDTH * 4)
    return pl.pallas_call(
        functools.partial(_attention_kernel, n_pages=g, n_chunks=n_chunks),
        grid_spec=grid_spec,
        out_shape=[jax.ShapeDtypeStruct((nb, t, width), BF16),
                   jax.ShapeDtypeStruct((bs, ATT_WIDTH, V7X_LANES), F32)],
        compiler_params=pltpu.CompilerParams(
            dimension_semantics=("arbitrary", "arbitrary", "arbitrary"),
            vmem_limit_bytes=_vmem_limit(est)),
        name="attention",
    )(page_order, bias, q_bf, k_bf, v_bf2, u_prompt, qbd, bcol, u_sample, knt, vnt, cache_kt, cache_vt)


def _merge_kernel(x_ref, o_ref, yc_ref, sga_ref, sgb_ref, gt1_ref, sh2_ref, sc2_ref, gt2_ref, g2_ref,
                  wa_ref, wc_ref, wo_ref, w1_ref, w2_ref, out_ref):
    ya = jnp.dot(o_ref[0], wa_ref[...], preferred_element_type=F32)
    yb = jnp.dot(yc_ref[0], wc_ref[...], preferred_element_type=F32)
    mixed = (sga_ref[0].astype(F32) * ya + sgb_ref[0].astype(F32) * yb).astype(BF16)
    x1 = x_ref[0] + gt1_ref[0] * jnp.dot(mixed, wo_ref[...], preferred_element_type=F32)
    ms = jnp.mean(x1 * x1, axis=-1, keepdims=True)
    h2 = x1 * lax.rsqrt(ms + EPS) * g2_ref[...]
    h2 = (h2 * (1.0 + sc2_ref[0]) + sh2_ref[0]).astype(BF16)
    chunk = D_MODEL
    mlp = jnp.zeros(x1.shape, F32)
    for c in range(D_FF // chunk):
        hid = jnp.maximum(jnp.dot(h2, w1_ref[:, c * chunk:(c + 1) * chunk], preferred_element_type=F32), 0.0)
        hid = (hid * hid).astype(BF16)
        mlp = mlp + jnp.dot(hid, w2_ref[c * chunk:(c + 1) * chunk, :], preferred_element_type=F32)
    out_ref[0] = x1 + gt2_ref[0] * mlp


def _merge(x, o_att, yc, sga, sgb, gt1, sh2, sc2, gt2, g2, wa, wc, wo, w1, w2, tm):
    nb, t, d = x.shape
    nt = t // tm
    mod_rows = gt1.shape[1]
    mod_block = (1, tm, d) if mod_rows == t else (1, 1, d)
    mod_map = (lambda b, i: (b, i, 0)) if mod_rows == t else (lambda b, i: (b, 0, 0))
    row = lambda w: pl.BlockSpec((1, tm, w), lambda b, i: (b, i, 0))
    mod = pl.BlockSpec(mod_block, mod_map)
    const = lambda shape: _resident(shape, lambda b, i: (0,) * len(shape))
    w_bytes = 2 * (2 * ATT_WIDTH * d + d * d + 2 * d * D_FF)
    est = w_bytes + 2 * tm * (4 * d + 2 * ATT_WIDTH * 2 + 2 * d * 2 + 4 * d) + 8 * tm * d * 4
    return pl.pallas_call(
        _merge_kernel,
        grid=(nb, nt),
        in_specs=[row(d), row(ATT_WIDTH), row(CONV_CH), row(d), row(d), mod, mod, mod, mod, const((1, d)),
                  const((ATT_WIDTH, d)), const((CONV_CH, d)), const((d, d)), const((d, D_FF)), const((D_FF, d))],
        out_specs=row(d),
        out_shape=jax.ShapeDtypeStruct((nb, t, d), F32),
        compiler_params=pltpu.CompilerParams(
            dimension_semantics=("arbitrary", "arbitrary"),
            vmem_limit_bytes=_vmem_limit(est)),
        name="merge_mlp",
    )(x, o_att, yc, sga, sgb, gt1, sh2, sc2, gt2, g2, wa, wc, wo, w1, w2)


def _neg_suffix_matrix(n):
    j = jnp.arange(n)[:, None]
    s = jnp.arange(n)[None, :]
    return jnp.where(j >= s, -1.0, 0.0).astype(BF16)


def _slot_minor_pages(cache):
    n_phys, slots, heads, dim = cache.shape
    return jnp.transpose(cache, (0, 2, 3, 1)).reshape(n_phys, heads * dim, slots)


def kernel(x_prompt, x_sample, c_prompt, c_sample, cache_k, cache_v, state_conv, page_table, rms_g1, rms_g2,
           w_ada, b_ada, w_in, q_norm_g, k_norm_g, sb_bias, conv_w, conv_b, w_att_out, w_conv_out, w_o, w_mlp1,
           w_mlp2):
    depth = w_in.shape[0]
    assert depth == 1, "single-layer step"
    l = 0
    bp, seq, d = x_prompt.shape
    bs, dec, _ = x_sample.shape
    assert dec * N_HEADS == SAMPLE_ROWS

    w_in_bf = w_in[l].astype(BF16)
    wa, wc, wo = w_att_out[l].astype(BF16), w_conv_out[l].astype(BF16), w_o[l].astype(BF16)
    w1, w2 = w_mlp1[l].astype(BF16), w_mlp2[l].astype(BF16)
    g1 = rms_g1[l].reshape(1, d)
    g2 = rms_g2[l].reshape(1, d)
    qg = jnp.tile(q_norm_g[l], N_HEADS).reshape(1, ATT_WIDTH)
    kg = jnp.tile(k_norm_g[l], N_HEADS).reshape(1, ATT_WIDTH)
    cw = conv_w[l]
    cb = conv_b[l].reshape(1, CONV_CH)
    grp = jnp.arange(V7X_MXU_DIM) // HEAD_DIM
    gmat = jnp.where(grp[:, None] == grp[None, :], 1.0 / HEAD_DIM, 0.0).astype(BF16)
    bias = sb_bias[l].astype(F32) * LOG2E

    mod = _ada(jnp.concatenate([c_prompt, c_sample], axis=0), w_ada[l], b_ada[l])
    mods = [mod[:, j * d:(j + 1) * d] for j in range(N_MOD)]
    mp = [m[:bp].reshape(bp, 1, d) for m in mods]
    ms_ = [jnp.repeat(m[bp:], dec, axis=0).reshape(1, bs * dec, d) for m in mods]

    (q_p, k_p, kb_p, v_p, vb_p, yc_p, sga_p, sgb_p, cs_p) = _proj(
        x_prompt, mp[0], mp[1], g1, w_in_bf, qg, kg, gmat, cw, cb)

    rows = bs * dec
    xs = x_sample.reshape(1, rows, d)
    st = state_conv[l]
    step = jnp.tile(jnp.arange(dec), bs)[:, None]
    flag1 = jnp.broadcast_to(step < 1, (rows, CONV_CH)).astype(F32)
    flag2 = jnp.broadcast_to(step < 2, (rows, CONV_CH)).astype(F32)
    zero = jnp.zeros((bs, dec - 2, CONV_CH), F32)
    val1 = jnp.concatenate([st[:, 1:2], zero, zero[:, :1]], axis=1).reshape(rows, CONV_CH)
    val2 = jnp.concatenate([st, zero], axis=1).reshape(rows, CONV_CH)
    hist = (jnp.stack([flag1, val1]), jnp.stack([flag2, val2]))
    (q_s, k_s, _, v_s, _, yc_s, sga_s, sgb_s, u_s) = _proj(
        xs, ms_[0], ms_[1], g1, w_in_bf, qg, kg, gmat, cw, cb, hist=hist)

    q4 = q_s.reshape(bs, dec, N_HEADS, HEAD_DIM)
    eye = jnp.eye(N_HEADS, dtype=BF16)
    qbd = jnp.einsum('bthd,hg->bthgd', q4, eye).reshape(bs, SAMPLE_ROWS, ATT_WIDTH)
    bcol = jnp.tile(bias, dec).reshape(SAMPLE_ROWS, 1)
    pad_slots = ((0, 0), (0, 0), (0, PAGE_SIZE - dec))
    knt = jnp.pad(jnp.transpose(k_s.reshape(ATT_WIDTH, bs, dec), (1, 0, 2)), pad_slots)
    vnt = jnp.pad(jnp.transpose(v_s.reshape(ATT_WIDTH, bs, dec), (1, 0, 2)), pad_slots)
    o_p, ot = _attention(q_p, kb_p, vb_p, bias, _neg_suffix_matrix(K_TILE),
                         page_table, qbd, bcol, _neg_suffix_matrix(PAGE_SIZE), knt, vnt,
                         _slot_minor_pages(cache_k[l]), _slot_minor_pages(cache_v[l]))
    y_p = _merge(x_prompt, o_p, yc_p, sga_p, sgb_p, mp[2], mp[3], mp[4], mp[5], g2, wa, wc, wo, w1, w2,
                 tm=ROW_TILE)
    ot = ot[:, :, :SAMPLE_ROWS].reshape(bs, N_HEADS, HEAD_DIM, dec, N_HEADS)
    o_s = jnp.einsum('bhdth->bthd', ot).reshape(1, rows, ATT_WIDTH).astype(BF16)
    y_s = _merge(xs, o_s, yc_s, sga_s, sgb_s, ms_[2], ms_[3], ms_[4], ms_[5], g2, wa, wc, wo, w1, w2, tm=rows)

    def seq_major(xt, nb_, t_):
        x4 = xt.reshape(xt.shape[0], N_HEADS, HEAD_DIM, -1)
        return jnp.transpose(x4, (0, 3, 1, 2)).reshape(1, nb_, t_, N_HEADS, HEAD_DIM)

    return (y_p,
            y_s.reshape(bs, dec, d),
            seq_major(k_p, bp, seq), seq_major(v_p, bp, seq),
            cs_p.reshape(1, bp, CONV_WIDTH - 1, CONV_CH),
            seq_major(k_s, bs, dec), seq_major(v_s, bs, dec),
            u_s.reshape(bs, dec, CONV_CH)[:, dec - (CONV_WIDTH - 1):].reshape(1, bs, CONV_WIDTH - 1, CONV_CH))
```

```python
import functools
import math

import jax
import jax.numpy as jnp
from jax import lax
from jax.experimental import pallas as pl
from jax.experimental.pallas import tpu as pltpu

D_MODEL = 1024
N_HEADS = 8
HEAD_DIM = 64
ATT_WIDTH = N_HEADS * HEAD_DIM
CONV_CH = 512
CONV_WIDTH = 3
D_FF = 4 * D_MODEL
N_MOD = 6
PAGE_SIZE = 128
EPS = 1e-6
SB_SCALE = 1.0 / math.sqrt(HEAD_DIM)
LOG2E = math.log2(math.e)
Q_SCALE = SB_SCALE * LOG2E
IN_WIDTH = 3 * ATT_WIDTH + 3 * CONV_CH + 2 * D_MODEL

V7X_LANES = 128
V7X_SUBLANES = 8
V7X_MXU_DIM = 256
V7X_VMEM_BYTES = 64 * 1024 * 1024

ROW_TILE = 512
K_TILE = V7X_MXU_DIM
Q_TILE = K_TILE
PAGES_PER_STEP = 16
PAGE_BUFFERS = 2
HIST_ROWS = V7X_SUBLANES
SOFTPLUS2_LINEAR_FROM = 60.0
MASKED_LOG_WEIGHT = -1e30

F32 = jnp.float32
BF16 = jnp.bfloat16


def _vmem_limit(nbytes):
    return int(min(nbytes + (8 << 20), V7X_VMEM_BYTES - (6 << 20)))


def _resident(shape, index_map):
    return pl.BlockSpec(shape, index_map, pipeline_mode=pl.Buffered(1))


def _mm(lhs, rhs):
    return lax.dot_general(lhs, rhs, (((1,), (0,)), ((), ())), preferred_element_type=F32)


def _sigmoid(x):
    return 0.5 * jnp.tanh(0.5 * x) + 0.5


def _softplus2(z2):
    return jnp.log(1.0 + jnp.exp2(z2)) * LOG2E


def _ada_kernel(c_ref, w_ref, b_ref, o_ref):
    c = c_ref[...]
    s = (c * jax.nn.sigmoid(c)).astype(BF16)
    o_ref[...] = jnp.dot(s, w_ref[...], preferred_element_type=F32) + b_ref[...]


def _ada(c_all, w_ada, b_ada):
    n, d = c_all.shape
    width = w_ada.shape[1]
    tn = 1536
    return pl.pallas_call(
        _ada_kernel,
        grid=(width // tn,),
        in_specs=[pl.BlockSpec((n, d), lambda j: (0, 0)),
                  pl.BlockSpec((d, tn), lambda j: (0, j)),
                  pl.BlockSpec((1, tn), lambda j: (0, j))],
        out_specs=pl.BlockSpec((n, tn), lambda j: (0, j)),
        out_shape=jax.ShapeDtypeStruct((n, width), F32),
        compiler_params=pltpu.CompilerParams(
            dimension_semantics=("arbitrary",),
            vmem_limit_bytes=_vmem_limit(2 * d * tn * 4)),
        name="ada",
    )(c_all, w_ada, b_ada.reshape(1, width))


def _head_rms(p, gmat_ref, gain):
    sq = (p * p).astype(BF16)
    parts = []
    for c in range(ATT_WIDTH // V7X_MXU_DIM):
        sl = slice(c * V7X_MXU_DIM, (c + 1) * V7X_MXU_DIM)
        parts.append(jnp.dot(sq[:, sl], gmat_ref[...], preferred_element_type=F32))
    ms = jnp.concatenate(parts, axis=-1)
    return p * lax.rsqrt(ms + EPS) * gain


def _proj_kernel(*refs, tm, sample_mode):
    if sample_mode:
        (x_ref, sh_ref, sc_ref, g1_ref, w_ref, qg_ref, kg_ref, gmat_ref, cw_ref, cb_ref, h1_ref, h2_ref,
         q_out, k_out, kb_out, v_out, vb_out, yc_out, sga_out, sgb_out, u_out, conv_buf) = refs
    else:
        (x_ref, sh_ref, sc_ref, g1_ref, w_ref, qg_ref, kg_ref, gmat_ref, cw_ref, cb_ref,
         q_out, k_out, kb_out, v_out, vb_out, yc_out, sga_out, sgb_out, cs_out, conv_buf) = refs

        @pl.when((pl.program_id(0) == 0) & (pl.program_id(1) == 0))
        def _():
            conv_buf[0:HIST_ROWS, :] = jnp.zeros((HIST_ROWS, CONV_CH), F32)

    x = x_ref[0]
    ms = jnp.mean(x * x, axis=-1, keepdims=True)
    h = x * lax.rsqrt(ms + EPS) * g1_ref[...]
    h = (h * (1.0 + sc_ref[0]) + sh_ref[0]).astype(BF16)

    def part(lo, width):
        return jnp.dot(h, w_ref[:, lo:lo + width], preferred_element_type=F32)

    a = ATT_WIDTH
    c0 = 3 * a
    hc = part(c0, CONV_CH)
    bgate = part(c0 + CONV_CH, CONV_CH)
    cgate = part(c0 + 2 * CONV_CH, CONV_CH)
    u = cgate * hc

    g0 = c0 + 3 * CONV_CH
    sga_out[0] = _sigmoid(part(g0, D_MODEL)).astype(BF16)
    sgb_out[0] = _sigmoid(part(g0 + D_MODEL, D_MODEL)).astype(BF16)

    if sample_mode:
        u_out[0] = u
        prev = jnp.zeros((HIST_ROWS, CONV_CH), F32)
    else:
        first_tile = pl.program_id(1) == 0
        prev = jnp.where(first_tile, 0.0, conv_buf[0:HIST_ROWS, :])
    conv_buf[0:HIST_ROWS, :] = prev
    conv_buf[HIST_ROWS:HIST_ROWS + tm, :] = u
    u1 = conv_buf[HIST_ROWS - 1:HIST_ROWS - 1 + tm, :]
    u2 = conv_buf[HIST_ROWS - 2:HIST_ROWS - 2 + tm, :]
    if sample_mode:
        u1 = jnp.where(h1_ref[0] != 0, h1_ref[1], u1)
        u2 = jnp.where(h2_ref[0] != 0, h2_ref[1], u2)
    y = cb_ref[...] + cw_ref[0:1, :] * u2 + cw_ref[1:2, :] * u1 + cw_ref[2:3, :] * u
    yc_out[0] = (bgate * y).astype(BF16)
    if not sample_mode:
        conv_buf[0:HIST_ROWS, :] = u[tm - HIST_ROWS:, :]
        cs_out[0] = u[tm - (CONV_WIDTH - 1):, :]

    qn = _head_rms(part(0, a), gmat_ref, qg_ref[...])
    q_out[0] = (qn * Q_SCALE).astype(BF16)
    kn = _head_rms(part(a, a), gmat_ref, kg_ref[...])
    k_out[0] = kn.T
    kb_out[0] = kn.astype(BF16)
    v = part(2 * a, a)
    v_out[0] = v.T
    parity = (lax.broadcasted_iota(jnp.int32, v.shape, 1) // HEAD_DIM) % 2
    for p in range(2):
        vb_out[0, :, p * a:(p + 1) * a] = jnp.where(parity == p, v, 0.0).astype(BF16)


def _proj(x, sh, sc, g1, w_in_bf, qg, kg, gmat, conv_w, conv_b, hist=None):
    nb, t, d = x.shape
    sample_mode = hist is not None
    tm = t if sample_mode else ROW_TILE
    nt = t // tm
    mod_rows = sh.shape[1]
    mod_block = (1, tm, d) if mod_rows == t else (1, 1, d)
    mod_map = (lambda b, i: (b, i, 0)) if mod_rows == t else (lambda b, i: (b, 0, 0))
    row = lambda w: pl.BlockSpec((1, tm, w), lambda b, i: (b, i, 0))
    const = lambda shape: _resident(shape, lambda b, i: (0,) * len(shape))

    in_specs = [row(d), pl.BlockSpec(mod_block, mod_map), pl.BlockSpec(mod_block, mod_map),
                const((1, d)), const((d, IN_WIDTH)), const((1, ATT_WIDTH)), const((1, ATT_WIDTH)),
                const((V7X_MXU_DIM, V7X_MXU_DIM)), const((CONV_WIDTH, CONV_CH)), const((1, CONV_CH))]
    args = [x, sh, sc, g1, w_in_bf, qg, kg, gmat, conv_w, conv_b]
    out_shape = [jax.ShapeDtypeStruct((nb, t, ATT_WIDTH), BF16),
                 jax.ShapeDtypeStruct((nb, ATT_WIDTH, t), F32),
                 jax.ShapeDtypeStruct((nb, t, ATT_WIDTH), BF16),
                 jax.ShapeDtypeStruct((nb, ATT_WIDTH, t), F32),
                 jax.ShapeDtypeStruct((nb, t, 2 * ATT_WIDTH), BF16),
                 jax.ShapeDtypeStruct((nb, t, CONV_CH), BF16),
                 jax.ShapeDtypeStruct((nb, t, D_MODEL), BF16),
                 jax.ShapeDtypeStruct((nb, t, D_MODEL), BF16)]
    col = pl.BlockSpec((1, ATT_WIDTH, tm), lambda b, i: (b, 0, i))
    out_specs = [row(ATT_WIDTH), col, row(ATT_WIDTH), col,
                 row(2 * ATT_WIDTH), row(CONV_CH), row(D_MODEL), row(D_MODEL)]
    if sample_mode:
        in_specs += [const((2, t, CONV_CH)), const((2, t, CONV_CH))]
        args += list(hist)
        out_shape.append(jax.ShapeDtypeStruct((nb, t, CONV_CH), F32))
        out_specs.append(row(CONV_CH))
    else:
        out_shape.append(jax.ShapeDtypeStruct((nb, CONV_WIDTH - 1, CONV_CH), F32))
        out_specs.append(pl.BlockSpec((1, CONV_WIDTH - 1, CONV_CH), lambda b, i: (b, 0, 0)))

    est = (d * IN_WIDTH * 2 + 2 * tm * d * 4 + 2 * tm * (ATT_WIDTH * 16 + CONV_CH * 6 + D_MODEL * 4)
           + 6 * tm * D_MODEL * 4)
    return pl.pallas_call(
        functools.partial(_proj_kernel, tm=tm, sample_mode=sample_mode),
        grid=(nb, nt),
        in_specs=in_specs,
        out_specs=out_specs,
        out_shape=out_shape,
        scratch_shapes=[pltpu.VMEM((tm + HIST_ROWS, CONV_CH), F32)],
        compiler_params=pltpu.CompilerParams(
            dimension_semantics=("arbitrary", "arbitrary"),
            vmem_limit_bytes=_vmem_limit(est)),
        name="proj_sample" if sample_mode else "proj_prompt",
    )(*args)


def _prompt_attention_part(i, part, n_parts, bias_ref, q_ref, k_ref, v_ref, u_ref, o_ref,
                           qm_ref, z_buf, w_buf, carry_ref, acc_ref, side_work):
    tq = q_ref.shape[1]
    pair = V7X_LANES
    n_pairs = q_ref.shape[2] // pair
    per_pair = pair // HEAD_DIM

    def key_rows(j):
        return pl.ds(pl.multiple_of(j * K_TILE, K_TILE), K_TILE)

    pairs = range(n_pairs)

    def scores(j, slot, hps=pairs, hhs=range(per_pair)):
        for h in [hp * per_pair + hh for hp in hps for hh in hhs]:
            hp = h // per_pair
            kt = k_ref[0, key_rows(j), hp * pair:(hp + 1) * pair]
            z = lax.dot_general(qm_ref[h], kt, (((1,), (1,)), ((), ())), preferred_element_type=F32)
            z_buf[slot, h] = jnp.minimum(z + bias_ref[h], SOFTPLUS2_LINEAR_FROM)

    def log_weights(slot, diagonal, hps=pairs, hhs=range(per_pair)):
        for h in [hp * per_pair + hh for hp in hps for hh in hhs]:
            z = z_buf[slot, h]
            sp = _softplus2(z)
            if diagonal:
                row = lax.broadcasted_iota(jnp.int32, (tq, K_TILE), 0)
                col = lax.broadcasted_iota(jnp.int32, (tq, K_TILE), 1)
                valid = row > col
                sp = jnp.where(valid, sp, 0.0)
            p = _mm(sp, u_ref[...])
            carry = carry_ref[h]
            w = z + p + jnp.concatenate([carry] * (K_TILE // pair), axis=1)
            if diagonal:
                w = jnp.where(valid, w, MASKED_LOG_WEIGHT)
            w_buf[slot, h] = w
            carry_ref[h] = carry + jnp.broadcast_to(p[:, 0:1], carry.shape)

    def weighted_values(j, slot, hps=pairs):
        width = n_pairs * pair
        for hp in hps:
            out = None
            for hh in range(per_pair):
                vh = v_ref[0, key_rows(j), hh * width + hp * pair:hh * width + (hp + 1) * pair]
                contrib = _mm(jnp.exp2(w_buf[slot, hp * per_pair + hh]), vh)
                out = contrib if out is None else out + contrib
            acc_ref[hp] += out

    @pl.when((part > 0) & (part < n_parts - 1))
    def _():
        side_work()

    @pl.when(part == 0)
    def _():
        lane_q = lax.broadcasted_iota(jnp.int32, (tq, pair), 1)
        for hp in pairs:
            q2 = q_ref[0, :, hp * pair:(hp + 1) * pair]
            for hh in range(per_pair):
                qm_ref[hp * per_pair + hh] = jnp.where(lane_q // HEAD_DIM == hh, q2, jnp.zeros_like(q2))
        carry_ref[...] = jnp.zeros(carry_ref.shape, F32)
        acc_ref[...] = jnp.zeros(acc_ref.shape, F32)
        scores(i, 0)
        scores(jnp.maximum(i - 1, 0), 1)
        side_work()
        log_weights(0, diagonal=True)

    def step(n, slot):
        for hp in pairs:
            log_weights(1 - slot, False, [hp])
            scores(i - n - 2, slot, [hp])
            weighted_values(i - n, slot, [hp])

    def two_steps(m, _):
        step(2 * m, 0)
        step(2 * m + 1, 1)
        return 0

    n_full = jnp.maximum(i - 1, 0)
    n_loop = n_full // 2
    lax.fori_loop(n_loop * part // n_parts, n_loop * (part + 1) // n_parts, two_steps, 0)

    @pl.when(part == n_parts - 1)
    def _():
        @pl.when(n_full % 2 == 1)
        def _():
            step(n_full - 1, 0)

        @pl.when(i == 0)
        def _():
            side_work()

        @pl.when(i >= 1)
        def _():
            side_work()
            slot = (i - 1) % 2
            weighted_values(1, slot)
            log_weights(1 - slot, False)

        weighted_values(0, i % 2)
        for hp in pairs:
            o_ref[0, :, hp * pair:(hp + 1) * pair] = acc_ref[hp].astype(o_ref.dtype)


SAMPLE_ROWS = 32
QUAD = V7X_LANES // SAMPLE_ROWS


def _sample_attention_chunk(qbd_ref, bcol_ref, u_ref, knt_ref, vnt_ref, kt_refs, vt_refs, o_ref, acc_ref, carry_ref):
    n_pages = len(kt_refs)
    rows = SAMPLE_ROWS
    qbd = qbd_ref[0]
    bcol = bcol_ref[...]
    col_block = lax.broadcasted_iota(jnp.int32, (PAGE_SIZE, V7X_LANES), 1) // rows

    def scores(kt):
        z2 = jnp.dot(qbd, kt, preferred_element_type=F32) + bcol
        return jnp.minimum(z2, SOFTPLUS2_LINEAR_FROM)

    def suffix(sp):
        return jnp.dot(sp.astype(BF16), u_ref[...], preferred_element_type=F32)

    def transposed_weights(a_blocks):
        at = jnp.concatenate(a_blocks, axis=0).T
        return [jnp.where(col_block == p, at, 0.0) for p in range(len(a_blocks))]

    def new_keys(seq):
        n_q = rows // N_HEADS
        z = scores(knt_ref[0])
        key = lax.broadcasted_iota(jnp.int32, (rows, PAGE_SIZE), 1)
        qry = lax.broadcasted_iota(jnp.int32, (rows, PAGE_SIZE), 0) // N_HEADS
        valid = (key // n_q == seq) & (key % n_q < qry)
        sp = jnp.where(valid, _softplus2(z), 0.0)
        p = suffix(sp)
        a = jnp.where(valid, jnp.exp2(z + p), 0.0)
        zero = jnp.zeros_like(a)
        w = transposed_weights([a] + [zero] * (QUAD - 1))[0]
        acc_ref[...] = jnp.dot(vnt_ref[0], w, preferred_element_type=F32)
        carry_ref[...] = p[:, 0:1]

    def cached_pages():
        zs = [scores(kt_refs[g][...]) for g in range(n_pages)]
        z_all = jnp.concatenate(zs, axis=0)
        p_all = suffix(_softplus2(z_all))
        carry = carry_ref[...]
        carries = [None] * n_pages
        for g in reversed(range(n_pages)):
            carries[g] = carry
            carry = carry + p_all[g * rows:(g + 1) * rows, 0:1]
        carry_ref[...] = carry
        a_all = jnp.exp2(z_all + p_all + jnp.concatenate(carries, axis=0))

        acc = acc_ref[...]
        for q0 in range(0, n_pages, QUAD):
            w = transposed_weights([a_all[(q0 + p) * rows:(q0 + p + 1) * rows, :] for p in range(QUAD)])
            for p in range(0, QUAD, 2):
                vt2 = jnp.concatenate([vt_refs[q0 + p][...], vt_refs[q0 + p + 1][...]], axis=1)
                w2 = jnp.concatenate([w[p], w[p + 1]], axis=0)
                acc = acc + jnp.dot(vt2, w2, preferred_element_type=F32)
        acc_ref[...] = acc

    def finish():
        acc = acc_ref[...]
        total = acc
        for p in range(1, QUAD):
            total = total + pltpu.roll(acc, p * rows, axis=1)
        o_ref[0] = total

    return new_keys, cached_pages, finish


PROMPT_PARTS = 4


def _attention_kernel(order_ref, bias_ref, q_ref, k_ref, v_ref, up_ref, qbd_ref, bcol_ref, us_ref, knt_ref, vnt_ref,
                      ck_hbm, cv_hbm, o_ref, os_ref, qm_ref, z_buf, w_buf, carry_ref, acc_ref, s_acc_ref,
                      s_carry_ref, page_buf, page_sem, *, n_pages, n_chunks):
    i = pl.program_id(1)
    part = pl.program_id(2)
    n_steps = pl.num_programs(0) * pl.num_programs(1) * PROMPT_PARTS
    step = (pl.program_id(0) * pl.num_programs(1) + i) * PROMPT_PARTS + part
    half = step % 2

    def page_copies(s, into):
        copies = []
        for g in range(n_pages):
            page = order_ref[s, g]
            copies.append(pltpu.make_async_copy(ck_hbm.at[page], page_buf.at[into, 0, g], page_sem.at[into]))
            copies.append(pltpu.make_async_copy(cv_hbm.at[page], page_buf.at[into, 1, g], page_sem.at[into]))
        return copies

    @pl.when(step == 0)
    def _():
        for copy in page_copies(0, 0):
            copy.start()

    @pl.when(step + 1 < n_steps)
    def _():
        for copy in page_copies(step + 1, 1 - half):
            copy.start()

    for copy in page_copies(step, half):
        copy.wait()
    kt_refs = [page_buf.at[half, 0, g] for g in range(n_pages)]
    vt_refs = [page_buf.at[half, 1, g] for g in range(n_pages)]
    new_keys, cached_pages, finish = _sample_attention_chunk(
        qbd_ref, bcol_ref, us_ref, knt_ref, vnt_ref, kt_refs, vt_refs, os_ref, s_acc_ref, s_carry_ref)
    chunk = step % n_chunks
    pl.when(chunk == 0)(functools.partial(new_keys, step // n_chunks))
    _prompt_attention_part(i, part, PROMPT_PARTS, bias_ref, q_ref, k_ref, v_ref, up_ref, o_ref,
                           qm_ref, z_buf, w_buf, carry_ref, acc_ref, cached_pages)
    pl.when(chunk == n_chunks - 1)(finish)


def _attention(q_bf, k_bf, v_bf2, bias, u_prompt, page_table, qbd, bcol, u_sample, knt, vnt, cache_kt, cache_vt):
    nb, t, width = q_bf.shape
    bs, n_tab = page_table.shape
    tq = Q_TILE
    nq = t // tq
    pair = V7X_LANES
    g = PAGES_PER_STEP
    n_chunks = n_tab // g
    rows = SAMPLE_ROWS
    assert qbd.shape[1] == rows and g % QUAD == 0
    assert nb * nq * PROMPT_PARTS == bs * n_chunks, "one page chunk per grid step"
    assert knt.shape == vnt.shape == (1, ATT_WIDTH, PAGE_SIZE), "new keys of all sequences: one page of columns"
    page_block = (1, ATT_WIDTH, PAGE_SIZE)
    page_order = page_table.reshape(bs, n_chunks, g)[:, ::-1, :].reshape(bs * n_chunks, g)

    def seq_of(b, i, c):
        return ((b * nq + i) * PROMPT_PARTS + c) // n_chunks

    per_seq = lambda shape: pl.BlockSpec(shape, lambda b, i, c, po, bs_: (seq_of(b, i, c), 0, 0))
    const = lambda shape: pl.BlockSpec(shape, lambda b, i, c, po, bs_: (0,) * len(shape))
    in_specs = [pl.BlockSpec((1, tq, width), lambda b, i, c, po, bs_: (b, i, 0)),
                _resident((1, t, width), lambda b, i, c, po, bs_: (b, 0, 0)),
                _resident((1, t, 2 * width), lambda b, i, c, po, bs_: (b, 0, 0)),
                _resident((K_TILE, K_TILE), lambda b, i, c, po, bs_: (0, 0)),
                per_seq((1, rows, ATT_WIDTH)), const((rows, 1)), const((PAGE_SIZE, PAGE_SIZE)),
                const(page_block), const(page_block),
                pl.BlockSpec(memory_space=pl.ANY), pl.BlockSpec(memory_space=pl.ANY)]
    tile_bytes = tq * K_TILE * 4
    scratch = [pltpu.VMEM((N_HEADS, tq, pair), BF16),
               pltpu.VMEM((2, N_HEADS, tq, K_TILE), F32),
               pltpu.VMEM((2, N_HEADS, tq, K_TILE), F32),
               pltpu.VMEM((N_HEADS, tq, pair), F32),
               pltpu.VMEM((width // pair, tq, pair), F32),
               pltpu.VMEM((ATT_WIDTH, V7X_LANES), F32),
               pltpu.VMEM((rows, 1), F32),
               pltpu.VMEM((PAGE_BUFFERS, 2, g, ATT_WIDTH, PAGE_SIZE), F32),
               pltpu.SemaphoreType.DMA((PAGE_BUFFERS,))]
    grid_spec = pltpu.PrefetchScalarGridSpec(
        num_scalar_prefetch=2,
        grid=(nb, nq, PROMPT_PARTS),
        in_specs=in_specs,
        out_specs=[pl.BlockSpec((1, tq, width), lambda b, i, c, po, bs_: (b, i, 0)),
                   pl.BlockSpec((1, ATT_WIDTH, V7X_LANES), lambda b, i, c, po, bs_: (seq_of(b, i, c), 0, 0))],
        scratch_shapes=scratch)
    est = (3 * t * width * 2 + 4 * tq * width * 2 + 4 * N_HEADS * tile_bytes
           + N_HEADS * tq * pair * (2 + 4) + 4 * tq * pair * 4 + 8 * tile_bytes
           + PAGE_BUFFERS * 2 * g * PAGE_SIZE * ATT_WIDTH * 4 + 6 * PAGE_SIZE * ATT_WIDTH * 4)
    return pl.pallas_call(
        functools.partial(_attention_kernel, n_pages=g, n_chunks=n_chunks),
        grid_spec=grid_spec,
        out_shape=[jax.ShapeDtypeStruct((nb, t, width), BF16),
                   jax.ShapeDtypeStruct((bs, ATT_WIDTH, V7X_LANES), F32)],
        compiler_params=pltpu.CompilerParams(
            dimension_semantics=("arbitrary", "arbitrary", "arbitrary"),
            vmem_limit_bytes=_vmem_limit(est)),
        name="attention",
    )(page_order, bias, q_bf, k_bf, v_bf2, u_prompt, qbd, bcol, u_sample, knt, vnt, cache_kt, cache_vt)


def _merge_kernel(x_ref, o_ref, yc_ref, sga_ref, sgb_ref, gt1_ref, sh2_ref, sc2_ref, gt2_ref, g2_ref,
                  wa_ref, wc_ref, wo_ref, w1_ref, w2_ref, out_ref):
    ya = jnp.dot(o_ref[0], wa_ref[...], preferred_element_type=F32)
    yb = jnp.dot(yc_ref[0], wc_ref[...], preferred_element_type=F32)
    mixed = (sga_ref[0].astype(F32) * ya + sgb_ref[0].astype(F32) * yb).astype(BF16)
    x1 = x_ref[0] + gt1_ref[0] * jnp.dot(mixed, wo_ref[...], preferred_element_type=F32)
    ms = jnp.mean(x1 * x1, axis=-1, keepdims=True)
    h2 = x1 * lax.rsqrt(ms + EPS) * g2_ref[...]
    h2 = (h2 * (1.0 + sc2_ref[0]) + sh2_ref[0]).astype(BF16)
    chunk = D_MODEL
    mlp = jnp.zeros(x1.shape, F32)
    for c in range(D_FF // chunk):
        hid = jnp.maximum(jnp.dot(h2, w1_ref[:, c * chunk:(c + 1) * chunk], preferred_element_type=F32), 0.0)
        hid = (hid * hid).astype(BF16)
        mlp = mlp + jnp.dot(hid, w2_ref[c * chunk:(c + 1) * chunk, :], preferred_element_type=F32)
    out_ref[0] = x1 + gt2_ref[0] * mlp


def _merge(x, o_att, yc, sga, sgb, gt1, sh2, sc2, gt2, g2, wa, wc, wo, w1, w2, tm):
    nb, t, d = x.shape
    nt = t // tm
    mod_rows = gt1.shape[1]
    mod_block = (1, tm, d) if mod_rows == t else (1, 1, d)
    mod_map = (lambda b, i: (b, i, 0)) if mod_rows == t else (lambda b, i: (b, 0, 0))
    row = lambda w: pl.BlockSpec((1, tm, w), lambda b, i: (b, i, 0))
    mod = pl.BlockSpec(mod_block, mod_map)
    const = lambda shape: _resident(shape, lambda b, i: (0,) * len(shape))
    w_bytes = 2 * (2 * ATT_WIDTH * d + d * d + 2 * d * D_FF)
    est = w_bytes + 2 * tm * (4 * d + 2 * ATT_WIDTH * 2 + 2 * d * 2 + 4 * d) + 8 * tm * d * 4
    return pl.pallas_call(
        _merge_kernel,
        grid=(nb, nt),
        in_specs=[row(d), row(ATT_WIDTH), row(CONV_CH), row(d), row(d), mod, mod, mod, mod, const((1, d)),
                  const((ATT_WIDTH, d)), const((CONV_CH, d)), const((d, d)), const((d, D_FF)), const((D_FF, d))],
        out_specs=row(d),
        out_shape=jax.ShapeDtypeStruct((nb, t, d), F32),
        compiler_params=pltpu.CompilerParams(
            dimension_semantics=("arbitrary", "arbitrary"),
            vmem_limit_bytes=_vmem_limit(est)),
        name="merge_mlp",
    )(x, o_att, yc, sga, sgb, gt1, sh2, sc2, gt2, g2, wa, wc, wo, w1, w2)


def _neg_suffix_matrix(n):
    j = jnp.arange(n)[:, None]
    s = jnp.arange(n)[None, :]
    return jnp.where(j >= s, -1.0, 0.0).astype(BF16)


def _slot_minor_pages(cache):
    n_phys, slots, heads, dim = cache.shape
    return jnp.transpose(cache, (0, 2, 3, 1)).reshape(n_phys, heads * dim, slots)


def kernel(x_prompt, x_sample, c_prompt, c_sample, cache_k, cache_v, state_conv, page_table, rms_g1, rms_g2,
           w_ada, b_ada, w_in, q_norm_g, k_norm_g, sb_bias, conv_w, conv_b, w_att_out, w_conv_out, w_o, w_mlp1,
           w_mlp2):
    depth = w_in.shape[0]
    assert depth == 1, "single-layer step"
    l = 0
    bp, seq, d = x_prompt.shape
    bs, dec, _ = x_sample.shape
    assert dec * N_HEADS == SAMPLE_ROWS

    w_in_bf = w_in[l].astype(BF16)
    wa, wc, wo = w_att_out[l].astype(BF16), w_conv_out[l].astype(BF16), w_o[l].astype(BF16)
    w1, w2 = w_mlp1[l].astype(BF16), w_mlp2[l].astype(BF16)
    g1 = rms_g1[l].reshape(1, d)
    g2 = rms_g2[l].reshape(1, d)
    qg = jnp.tile(q_norm_g[l], N_HEADS).reshape(1, ATT_WIDTH)
    kg = jnp.tile(k_norm_g[l], N_HEADS).reshape(1, ATT_WIDTH)
    cw = conv_w[l]
    cb = conv_b[l].reshape(1, CONV_CH)
    grp = jnp.arange(V7X_MXU_DIM) // HEAD_DIM
    gmat = jnp.where(grp[:, None] == grp[None, :], 1.0 / HEAD_DIM, 0.0).astype(BF16)
    bias = sb_bias[l].astype(F32) * LOG2E

    mod = _ada(jnp.concatenate([c_prompt, c_sample], axis=0), w_ada[l], b_ada[l])
    mods = [mod[:, j * d:(j + 1) * d] for j in range(N_MOD)]
    mp = [m[:bp].reshape(bp, 1, d) for m in mods]
    ms_ = [jnp.repeat(m[bp:], dec, axis=0).reshape(1, bs * dec, d) for m in mods]

    (q_p, k_p, kb_p, v_p, vb_p, yc_p, sga_p, sgb_p, cs_p) = _proj(
        x_prompt, mp[0], mp[1], g1, w_in_bf, qg, kg, gmat, cw, cb)

    rows = bs * dec
    xs = x_sample.reshape(1, rows, d)
    st = state_conv[l]
    step = jnp.tile(jnp.arange(dec), bs)[:, None]
    flag1 = jnp.broadcast_to(step < 1, (rows, CONV_CH)).astype(F32)
    flag2 = jnp.broadcast_to(step < 2, (rows, CONV_CH)).astype(F32)
    zero = jnp.zeros((bs, dec - 2, CONV_CH), F32)
    val1 = jnp.concatenate([st[:, 1:2], zero, zero[:, :1]], axis=1).reshape(rows, CONV_CH)
    val2 = jnp.concatenate([st, zero], axis=1).reshape(rows, CONV_CH)
    hist = (jnp.stack([flag1, val1]), jnp.stack([flag2, val2]))
    (q_s, k_s, _, v_s, _, yc_s, sga_s, sgb_s, u_s) = _proj(
        xs, ms_[0], ms_[1], g1, w_in_bf, qg, kg, gmat, cw, cb, hist=hist)

    q4 = q_s.reshape(bs, dec, N_HEADS, HEAD_DIM)
    eye = jnp.eye(N_HEADS, dtype=BF16)
    qbd = jnp.einsum('bthd,hg->bthgd', q4, eye).reshape(bs, SAMPLE_ROWS, ATT_WIDTH)
    bcol = jnp.tile(bias, dec).reshape(SAMPLE_ROWS, 1)
    o_p, ot = _attention(q_p, kb_p, vb_p, bias, _neg_suffix_matrix(K_TILE),
                         page_table, qbd, bcol, _neg_suffix_matrix(PAGE_SIZE), k_s, v_s,
                         _slot_minor_pages(cache_k[l]), _slot_minor_pages(cache_v[l]))
    y_p = _merge(x_prompt, o_p, yc_p, sga_p, sgb_p, mp[2], mp[3], mp[4], mp[5], g2, wa, wc, wo, w1, w2,
                 tm=ROW_TILE)
    ot = ot[:, :, :SAMPLE_ROWS].reshape(bs, N_HEADS, HEAD_DIM, dec, N_HEADS)
    o_s = jnp.einsum('bhdth->bthd', ot).reshape(1, rows, ATT_WIDTH).astype(BF16)
    y_s = _merge(xs, o_s, yc_s, sga_s, sgb_s, ms_[2], ms_[3], ms_[4], ms_[5], g2, wa, wc, wo, w1, w2, tm=rows)

    def seq_major(xt, nb_, t_):
        x4 = xt.reshape(xt.shape[0], N_HEADS, HEAD_DIM, -1)
        return jnp.transpose(x4, (0, 3, 1, 2)).reshape(1, nb_, t_, N_HEADS, HEAD_DIM)

    return (y_p,
            y_s.reshape(bs, dec, d),
            seq_major(k_p, bp, seq), seq_major(v_p, bp, seq),
            cs_p.reshape(1, bp, CONV_WIDTH - 1, CONV_CH),
            seq_major(k_s, bs, dec), seq_major(v_s, bs, dec),
            u_s.reshape(bs, dec, CONV_CH)[:, dec - (CONV_WIDTH - 1):].reshape(1, bs, CONV_WIDTH - 1, CONV_CH))
```

```python
import functools
import math

import jax
import jax.numpy as jnp
from jax import lax
from jax.experimental import pallas as pl
from jax.experimental.pallas import tpu as pltpu

D_MODEL = 1024
N_HEADS = 8
HEAD_DIM = 64
ATT_WIDTH = N_HEADS * HEAD_DIM
CONV_CH = 512
CONV_WIDTH = 3
D_FF = 4 * D_MODEL
N_MOD = 6
PAGE_SIZE = 128
EPS = 1e-6
SB_SCALE = 1.0 / math.sqrt(HEAD_DIM)
LOG2E = math.log2(math.e)
Q_SCALE = SB_SCALE * LOG2E
IN_WIDTH = 3 * ATT_WIDTH + 3 * CONV_CH + 2 * D_MODEL

V7X_LANES = 128
V7X_SUBLANES = 8
V7X_MXU_DIM = 256
V7X_VMEM_BYTES = 64 * 1024 * 1024

ROW_TILE = 512
K_TILE = V7X_MXU_DIM
Q_TILE = K_TILE
PAGES_PER_STEP = 16
PAGE_BUFFERS = 2
HIST_ROWS = V7X_SUBLANES
SOFTPLUS2_LINEAR_FROM = 60.0
MASKED_LOG_WEIGHT = -1e30

F32 = jnp.float32
BF16 = jnp.bfloat16


def _vmem_limit(nbytes):
    return int(min(nbytes + (8 << 20), V7X_VMEM_BYTES - (6 << 20)))


def _resident(shape, index_map):
    return pl.BlockSpec(shape, index_map, pipeline_mode=pl.Buffered(1))


def _mm(lhs, rhs):
    return lax.dot_general(lhs, rhs, (((1,), (0,)), ((), ())), preferred_element_type=F32)


def _sigmoid(x):
    return 0.5 * jnp.tanh(0.5 * x) + 0.5


def _softplus2(z2):
    return jnp.log(1.0 + jnp.exp2(z2)) * LOG2E


def _ada_kernel(c_ref, w_ref, b_ref, o_ref):
    c = c_ref[...]
    s = (c * jax.nn.sigmoid(c)).astype(BF16)
    o_ref[...] = jnp.dot(s, w_ref[...], preferred_element_type=F32) + b_ref[...]


def _ada(c_all, w_ada, b_ada):
    n, d = c_all.shape
    width = w_ada.shape[1]
    tn = 1536
    return pl.pallas_call(
        _ada_kernel,
        grid=(width // tn,),
        in_specs=[pl.BlockSpec((n, d), lambda j: (0, 0)),
                  pl.BlockSpec((d, tn), lambda j: (0, j)),
                  pl.BlockSpec((1, tn), lambda j: (0, j))],
        out_specs=pl.BlockSpec((n, tn), lambda j: (0, j)),
        out_shape=jax.ShapeDtypeStruct((n, width), F32),
        compiler_params=pltpu.CompilerParams(
            dimension_semantics=("arbitrary",),
            vmem_limit_bytes=_vmem_limit(2 * d * tn * 4)),
        name="ada",
    )(c_all, w_ada, b_ada.reshape(1, width))


def _head_rms(p, gmat_ref, gain):
    sq = (p * p).astype(BF16)
    parts = []
    for c in range(ATT_WIDTH // V7X_MXU_DIM):
        sl = slice(c * V7X_MXU_DIM, (c + 1) * V7X_MXU_DIM)
        parts.append(jnp.dot(sq[:, sl], gmat_ref[...], preferred_element_type=F32))
    ms = jnp.concatenate(parts, axis=-1)
    return p * lax.rsqrt(ms + EPS) * gain


def _proj_kernel(*refs, tm, sample_mode):
    if sample_mode:
        (x_ref, sh_ref, sc_ref, g1_ref, w_ref, qg_ref, kg_ref, gmat_ref, cw_ref, cb_ref, h1_ref, h2_ref,
         q_out, k_out, kb_out, v_out, vb_out, yc_out, sga_out, sgb_out, u_out, conv_buf) = refs
    else:
        (x_ref, sh_ref, sc_ref, g1_ref, w_ref, qg_ref, kg_ref, gmat_ref, cw_ref, cb_ref,
         q_out, k_out, kb_out, v_out, vb_out, yc_out, sga_out, sgb_out, cs_out, conv_buf) = refs

        @pl.when((pl.program_id(0) == 0) & (pl.program_id(1) == 0))
        def _():
            conv_buf[0:HIST_ROWS, :] = jnp.zeros((HIST_ROWS, CONV_CH), F32)

    x = x_ref[0]
    ms = jnp.mean(x * x, axis=-1, keepdims=True)
    h = x * lax.rsqrt(ms + EPS) * g1_ref[...]
    h = (h * (1.0 + sc_ref[0]) + sh_ref[0]).astype(BF16)

    def part(lo, width):
        return _mm(h, w_ref[:, lo:lo + width])

    a = ATT_WIDTH
    c0 = 3 * a
    hc = part(c0, CONV_CH)
    bgate = part(c0 + CONV_CH, CONV_CH)
    cgate = part(c0 + 2 * CONV_CH, CONV_CH)
    u = cgate * hc

    g0 = c0 + 3 * CONV_CH
    sga_out[0] = _sigmoid(part(g0, D_MODEL)).astype(BF16)
    sgb_out[0] = _sigmoid(part(g0 + D_MODEL, D_MODEL)).astype(BF16)

    if sample_mode:
        u_out[0] = u
        prev = jnp.zeros((HIST_ROWS, CONV_CH), F32)
    else:
        first_tile = pl.program_id(1) == 0
        prev = jnp.where(first_tile, 0.0, conv_buf[0:HIST_ROWS, :])
    conv_buf[0:HIST_ROWS, :] = prev
    conv_buf[HIST_ROWS:HIST_ROWS + tm, :] = u
    u1 = conv_buf[HIST_ROWS - 1:HIST_ROWS - 1 + tm, :]
    u2 = conv_buf[HIST_ROWS - 2:HIST_ROWS - 2 + tm, :]
    if sample_mode:
        u1 = jnp.where(h1_ref[0] != 0, h1_ref[1], u1)
        u2 = jnp.where(h2_ref[0] != 0, h2_ref[1], u2)
    y = cb_ref[...] + cw_ref[0:1, :] * u2 + cw_ref[1:2, :] * u1 + cw_ref[2:3, :] * u
    yc_out[0] = (bgate * y).astype(BF16)
    if not sample_mode:
        conv_buf[0:HIST_ROWS, :] = u[tm - HIST_ROWS:, :]
        cs_out[0] = u[tm - (CONV_WIDTH - 1):, :]

    qn = _head_rms(part(0, a), gmat_ref, qg_ref[...])
    q_out[0] = (qn * Q_SCALE).astype(BF16)
    kn = _head_rms(part(a, a), gmat_ref, kg_ref[...])
    k_out[0] = kn.T
    kb_out[0] = kn.astype(BF16)
    v = part(2 * a, a)
    v_out[0] = v.T
    parity = (lax.broadcasted_iota(jnp.int32, v.shape, 1) // HEAD_DIM) % 2
    for p in range(2):
        vb_out[0, :, p * a:(p + 1) * a] = jnp.where(parity == p, v, 0.0).astype(BF16)


def _proj(x, sh, sc, g1, w_in_bf, qg, kg, gmat, conv_w, conv_b, hist=None):
    nb, t, d = x.shape
    sample_mode = hist is not None
    tm = t if sample_mode else ROW_TILE
    nt = t // tm
    mod_rows = sh.shape[1]
    mod_block = (1, tm, d) if mod_rows == t else (1, 1, d)
    mod_map = (lambda b, i: (b, i, 0)) if mod_rows == t else (lambda b, i: (b, 0, 0))
    row = lambda w: pl.BlockSpec((1, tm, w), lambda b, i: (b, i, 0))
    const = lambda shape: _resident(shape, lambda b, i: (0,) * len(shape))

    in_specs = [row(d), pl.BlockSpec(mod_block, mod_map), pl.BlockSpec(mod_block, mod_map),
                const((1, d)), const((d, IN_WIDTH)), const((1, ATT_WIDTH)), const((1, ATT_WIDTH)),
                const((V7X_MXU_DIM, V7X_MXU_DIM)), const((CONV_WIDTH, CONV_CH)), const((1, CONV_CH))]
    args = [x, sh, sc, g1, w_in_bf, qg, kg, gmat, conv_w, conv_b]
    out_shape = [jax.ShapeDtypeStruct((nb, t, ATT_WIDTH), BF16),
                 jax.ShapeDtypeStruct((nb, ATT_WIDTH, t), F32),
                 jax.ShapeDtypeStruct((nb, t, ATT_WIDTH), BF16),
                 jax.ShapeDtypeStruct((nb, ATT_WIDTH, t), F32),
                 jax.ShapeDtypeStruct((nb, t, 2 * ATT_WIDTH), BF16),
                 jax.ShapeDtypeStruct((nb, t, CONV_CH), BF16),
                 jax.ShapeDtypeStruct((nb, t, D_MODEL), BF16),
                 jax.ShapeDtypeStruct((nb, t, D_MODEL), BF16)]
    col = pl.BlockSpec((1, ATT_WIDTH, tm), lambda b, i: (b, 0, i))
    out_specs = [row(ATT_WIDTH), col, row(ATT_WIDTH), col,
                 row(2 * ATT_WIDTH), row(CONV_CH), row(D_MODEL), row(D_MODEL)]
    if sample_mode:
        in_specs += [const((2, t, CONV_CH)), const((2, t, CONV_CH))]
        args += list(hist)
        out_shape.append(jax.ShapeDtypeStruct((nb, t, CONV_CH), F32))
        out_specs.append(row(CONV_CH))
    else:
        out_shape.append(jax.ShapeDtypeStruct((nb, CONV_WIDTH - 1, CONV_CH), F32))
        out_specs.append(pl.BlockSpec((1, CONV_WIDTH - 1, CONV_CH), lambda b, i: (b, 0, 0)))

    est = (d * IN_WIDTH * 4 + 2 * tm * d * 4 + 2 * tm * (ATT_WIDTH * 16 + CONV_CH * 6 + D_MODEL * 4)
           + 6 * tm * D_MODEL * 4)
    return pl.pallas_call(
        functools.partial(_proj_kernel, tm=tm, sample_mode=sample_mode),
        grid=(nb, nt),
        in_specs=in_specs,
        out_specs=out_specs,
        out_shape=out_shape,
        scratch_shapes=[pltpu.VMEM((tm + HIST_ROWS, CONV_CH), F32)],
        compiler_params=pltpu.CompilerParams(
            dimension_semantics=("arbitrary", "arbitrary"),
            vmem_limit_bytes=_vmem_limit(est)),
        name="proj_sample" if sample_mode else "proj_prompt",
    )(*args)


def _prompt_attention_part(i, part, n_parts, bias_ref, q_ref, k_ref, v_ref, u_ref, o_ref,
                           qm_ref, z_buf, w_buf, carry_ref, acc_ref, side_work):
    tq = q_ref.shape[1]
    pair = V7X_LANES
    n_pairs = q_ref.shape[2] // pair
    per_pair = pair // HEAD_DIM

    def key_rows(j):
        return pl.ds(pl.multiple_of(j * K_TILE, K_TILE), K_TILE)

    pairs = range(n_pairs)

    def scores(j, slot, hps=pairs, hhs=range(per_pair)):
        for h in [hp * per_pair + hh for hp in hps for hh in hhs]:
            hp = h // per_pair
            kt = k_ref[0, key_rows(j), hp * pair:(hp + 1) * pair]
            z = lax.dot_general(qm_ref[h], kt, (((1,), (1,)), ((), ())), preferred_element_type=F32)
            z_buf[slot, h] = jnp.minimum(z + bias_ref[h], SOFTPLUS2_LINEAR_FROM)

    def log_weights(slot, diagonal, hps=pairs, hhs=range(per_pair)):
        for h in [hp * per_pair + hh for hp in hps for hh in hhs]:
            z = z_buf[slot, h]
            sp = _softplus2(z)
            if diagonal:
                row = lax.broadcasted_iota(jnp.int32, (tq, K_TILE), 0)
                col = lax.broadcasted_iota(jnp.int32, (tq, K_TILE), 1)
                valid = row > col
                sp = jnp.where(valid, sp, 0.0)
            p = _mm(sp, u_ref[...])
            carry = carry_ref[h]
            w = z + p + jnp.concatenate([carry] * (K_TILE // pair), axis=1)
            if diagonal:
                w = jnp.where(valid, w, MASKED_LOG_WEIGHT)
            w_buf[slot, h] = w
            carry_ref[h] = carry + jnp.broadcast_to(p[:, 0:1], carry.shape)

    def weighted_values(j, slot, hps=pairs):
        width = n_pairs * pair
        for hp in hps:
            out = None
            for hh in range(per_pair):
                vh = v_ref[0, key_rows(j), hh * width + hp * pair:hh * width + (hp + 1) * pair]
                contrib = _mm(jnp.exp2(w_buf[slot, hp * per_pair + hh]), vh)
                out = contrib if out is None else out + contrib
            acc_ref[hp] += out

    @pl.when((part > 0) & (part < n_parts - 1))
    def _():
        side_work()

    @pl.when(part == 0)
    def _():
        lane_q = lax.broadcasted_iota(jnp.int32, (tq, pair), 1)
        for hp in pairs:
            q2 = q_ref[0, :, hp * pair:(hp + 1) * pair]
            for hh in range(per_pair):
                qm_ref[hp * per_pair + hh] = jnp.where(lane_q // HEAD_DIM == hh, q2, jnp.zeros_like(q2))
        carry_ref[...] = jnp.zeros(carry_ref.shape, F32)
        acc_ref[...] = jnp.zeros(acc_ref.shape, F32)
        scores(i, 0)
        scores(jnp.maximum(i - 1, 0), 1)
        side_work()
        log_weights(0, diagonal=True)

    def step(n, slot):
        for hp in pairs:
            log_weights(1 - slot, False, [hp])
            scores(i - n - 2, slot, [hp])
            weighted_values(i - n, slot, [hp])

    def two_steps(m, _):
        step(2 * m, 0)
        step(2 * m + 1, 1)
        return 0

    n_full = jnp.maximum(i - 1, 0)
    n_loop = n_full // 2
    lax.fori_loop(n_loop * part // n_parts, n_loop * (part + 1) // n_parts, two_steps, 0)

    @pl.when(part == n_parts - 1)
    def _():
        @pl.when(n_full % 2 == 1)
        def _():
            step(n_full - 1, 0)

        @pl.when(i == 0)
        def _():
            side_work()

        @pl.when(i >= 1)
        def _():
            side_work()
            slot = (i - 1) % 2
            weighted_values(1, slot)
            log_weights(1 - slot, False)

        weighted_values(0, i % 2)
        for hp in pairs:
            o_ref[0, :, hp * pair:(hp + 1) * pair] = acc_ref[hp].astype(o_ref.dtype)


SAMPLE_ROWS = 32
QUAD = V7X_LANES // SAMPLE_ROWS


def _sample_attention_chunk(qbd_ref, bcol_ref, u_ref, knt_ref, vnt_ref, kt_refs, vt_refs, o_ref, acc_ref, carry_ref):
    n_pages = len(kt_refs)
    rows = SAMPLE_ROWS
    qbd = qbd_ref[0]
    bcol = bcol_ref[...]
    col_block = lax.broadcasted_iota(jnp.int32, (PAGE_SIZE, V7X_LANES), 1) // rows

    def scores(kt):
        z2 = jnp.dot(qbd, kt, preferred_element_type=F32) + bcol
        return jnp.minimum(z2, SOFTPLUS2_LINEAR_FROM)

    def suffix(sp):
        return jnp.dot(sp.astype(BF16), u_ref[...], preferred_element_type=F32)

    def transposed_weights(a_blocks):
        at = jnp.concatenate(a_blocks, axis=0).T
        return [jnp.where(col_block == p, at, 0.0) for p in range(len(a_blocks))]

    def new_keys(seq):
        n_q = rows // N_HEADS
        z = scores(knt_ref[0])
        key = lax.broadcasted_iota(jnp.int32, (rows, PAGE_SIZE), 1)
        qry = lax.broadcasted_iota(jnp.int32, (rows, PAGE_SIZE), 0) // N_HEADS
        valid = (key // n_q == seq) & (key % n_q < qry)
        sp = jnp.where(valid, _softplus2(z), 0.0)
        p = suffix(sp)
        a = jnp.where(valid, jnp.exp2(z + p), 0.0)
        zero = jnp.zeros_like(a)
        w = transposed_weights([a] + [zero] * (QUAD - 1))[0]
        acc_ref[...] = jnp.dot(vnt_ref[0], w, preferred_element_type=F32)
        carry_ref[...] = p[:, 0:1]

    def cached_pages():
        zs = [scores(kt_refs[g][...]) for g in range(n_pages)]
        z_all = jnp.concatenate(zs, axis=0)
        p_all = suffix(_softplus2(z_all))
        carry = carry_ref[...]
        carries = [None] * n_pages
        for g in reversed(range(n_pages)):
            carries[g] = carry
            carry = carry + p_all[g * rows:(g + 1) * rows, 0:1]
        carry_ref[...] = carry
        a_all = jnp.exp2(z_all + p_all + jnp.concatenate(carries, axis=0))

        acc = acc_ref[...]
        for q0 in range(0, n_pages, QUAD):
            w = transposed_weights([a_all[(q0 + p) * rows:(q0 + p + 1) * rows, :] for p in range(QUAD)])
            for p in range(0, QUAD, 2):
                vt2 = jnp.concatenate([vt_refs[q0 + p][...], vt_refs[q0 + p + 1][...]], axis=1)
                w2 = jnp.concatenate([w[p], w[p + 1]], axis=0)
                acc = acc + jnp.dot(vt2, w2, preferred_element_type=F32)
        acc_ref[...] = acc

    def finish():
        acc = acc_ref[...]
        total = acc
        for p in range(1, QUAD):
            total = total + pltpu.roll(acc, p * rows, axis=1)
        o_ref[0] = total

    return new_keys, cached_pages, finish


PROMPT_PARTS = 4


def _attention_kernel(order_ref, bias_ref, q_ref, k_ref, v_ref, up_ref, qbd_ref, bcol_ref, us_ref, knt_ref, vnt_ref,
                      ck_hbm, cv_hbm, o_ref, os_ref, qm_ref, z_buf, w_buf, carry_ref, acc_ref, s_acc_ref,
                      s_carry_ref, page_buf, page_sem, *, n_pages, n_chunks):
    i = pl.program_id(1)
    part = pl.program_id(2)
    n_steps = pl.num_programs(0) * pl.num_programs(1) * PROMPT_PARTS
    step = (pl.program_id(0) * pl.num_programs(1) + i) * PROMPT_PARTS + part
    half = step % 2

    def page_copies(s, into):
        copies = []
        for g in range(n_pages):
            page = order_ref[s, g]
            copies.append(pltpu.make_async_copy(ck_hbm.at[page], page_buf.at[into, 0, g], page_sem.at[into]))
            copies.append(pltpu.make_async_copy(cv_hbm.at[page], page_buf.at[into, 1, g], page_sem.at[into]))
        return copies

    @pl.when(step == 0)
    def _():
        for copy in page_copies(0, 0):
            copy.start()

    @pl.when(step + 1 < n_steps)
    def _():
        for copy in page_copies(step + 1, 1 - half):
            copy.start()

    for copy in page_copies(step, half):
        copy.wait()
    kt_refs = [page_buf.at[half, 0, g] for g in range(n_pages)]
    vt_refs = [page_buf.at[half, 1, g] for g in range(n_pages)]
    new_keys, cached_pages, finish = _sample_attention_chunk(
        qbd_ref, bcol_ref, us_ref, knt_ref, vnt_ref, kt_refs, vt_refs, os_ref, s_acc_ref, s_carry_ref)
    chunk = step % n_chunks
    pl.when(chunk == 0)(functools.partial(new_keys, step // n_chunks))
    _prompt_attention_part(i, part, PROMPT_PARTS, bias_ref, q_ref, k_ref, v_ref, up_ref, o_ref,
                           qm_ref, z_buf, w_buf, carry_ref, acc_ref, cached_pages)
    pl.when(chunk == n_chunks - 1)(finish)


def _attention(q_bf, k_bf, v_bf2, bias, u_prompt, page_table, qbd, bcol, u_sample, knt, vnt, cache_kt, cache_vt):
    nb, t, width = q_bf.shape
    bs, n_tab = page_table.shape
    tq = Q_TILE
    nq = t // tq
    pair = V7X_LANES
    g = PAGES_PER_STEP
    n_chunks = n_tab // g
    rows = SAMPLE_ROWS
    assert qbd.shape[1] == rows and g % QUAD == 0
    assert nb * nq * PROMPT_PARTS == bs * n_chunks, "one page chunk per grid step"
    assert knt.shape == vnt.shape == (1, ATT_WIDTH, PAGE_SIZE), "new keys of all sequences: one page of columns"
    page_block = (1, ATT_WIDTH, PAGE_SIZE)
    page_order = page_table.reshape(bs, n_chunks, g)[:, ::-1, :].reshape(bs * n_chunks, g)

    def seq_of(b, i, c):
        return ((b * nq + i) * PROMPT_PARTS + c) // n_chunks

    per_seq = lambda shape: pl.BlockSpec(shape, lambda b, i, c, po, bs_: (seq_of(b, i, c), 0, 0))
    const = lambda shape: pl.BlockSpec(shape, lambda b, i, c, po, bs_: (0,) * len(shape))
    in_specs = [pl.BlockSpec((1, tq, width), lambda b, i, c, po, bs_: (b, i, 0)),
                _resident((1, t, width), lambda b, i, c, po, bs_: (b, 0, 0)),
                _resident((1, t, 2 * width), lambda b, i, c, po, bs_: (b, 0, 0)),
                _resident((K_TILE, K_TILE), lambda b, i, c, po, bs_: (0, 0)),
                per_seq((1, rows, ATT_WIDTH)), const((rows, 1)), const((PAGE_SIZE, PAGE_SIZE)),
                const(page_block), const(page_block),
                pl.BlockSpec(memory_space=pl.ANY), pl.BlockSpec(memory_space=pl.ANY)]
    tile_bytes = tq * K_TILE * 4
    scratch = [pltpu.VMEM((N_HEADS, tq, pair), BF16),
               pltpu.VMEM((2, N_HEADS, tq, K_TILE), F32),
               pltpu.VMEM((2, N_HEADS, tq, K_TILE), F32),
               pltpu.VMEM((N_HEADS, tq, pair), F32),
               pltpu.VMEM((width // pair, tq, pair), F32),
               pltpu.VMEM((ATT_WIDTH, V7X_LANES), F32),
               pltpu.VMEM((rows, 1), F32),
               pltpu.VMEM((PAGE_BUFFERS, 2, g, ATT_WIDTH, PAGE_SIZE), F32),
               pltpu.SemaphoreType.DMA((PAGE_BUFFERS,))]
    grid_spec = pltpu.PrefetchScalarGridSpec(
        num_scalar_prefetch=2,
        grid=(nb, nq, PROMPT_PARTS),
        in_specs=in_specs,
        out_specs=[pl.BlockSpec((1, tq, width), lambda b, i, c, po, bs_: (b, i, 0)),
                   pl.BlockSpec((1, ATT_WIDTH, V7X_LANES), lambda b, i, c, po, bs_: (seq_of(b, i, c), 0, 0))],
        scratch_shapes=scratch)
    est = (3 * t * width * 2 + 4 * tq * width * 2 + 4 * N_HEADS * tile_bytes
           + N_HEADS * tq * pair * (2 + 4) + 4 * tq * pair * 4 + 8 * tile_bytes
           + PAGE_BUFFERS * 2 * g * PAGE_SIZE * ATT_WIDTH * 4 + 6 * PAGE_SIZE * ATT_WIDTH * 4)
    return pl.pallas_call(
        functools.partial(_attention_kernel, n_pages=g, n_chunks=n_chunks),
        grid_spec=grid_spec,
        out_shape=[jax.ShapeDtypeStruct((nb, t, width), BF16),
                   jax.ShapeDtypeStruct((bs, ATT_WIDTH, V7X_LANES), F32)],
        compiler_params=pltpu.CompilerParams(
            dimension_semantics=("arbitrary", "arbitrary", "arbitrary"),
            vmem_limit_bytes=_vmem_limit(est)),
        name="attention",
    )(page_order, bias, q_bf, k_bf, v_bf2, u_prompt, qbd, bcol, u_sample, knt, vnt, cache_kt, cache_vt)


def _merge_kernel(x_ref, o_ref, yc_ref, sga_ref, sgb_ref, gt1_ref, sh2_ref, sc2_ref, gt2_ref, g2_ref,
                  wa_ref, wc_ref, wo_ref, w1_ref, w2_ref, out_ref):
    ya = _mm(o_ref[0], wa_ref[...])
    yb = _mm(yc_ref[0], wc_ref[...])
    mixed = (sga_ref[0].astype(F32) * ya + sgb_ref[0].astype(F32) * yb).astype(BF16)
    x1 = x_ref[0] + gt1_ref[0] * _mm(mixed, wo_ref[...])
    ms = jnp.mean(x1 * x1, axis=-1, keepdims=True)
    h2 = x1 * lax.rsqrt(ms + EPS) * g2_ref[...]
    h2 = (h2 * (1.0 + sc2_ref[0]) + sh2_ref[0]).astype(BF16)
    chunk = D_MODEL
    mlp = jnp.zeros(x1.shape, F32)
    for c in range(D_FF // chunk):
        hid = jnp.maximum(jnp.dot(h2, w1_ref[:, c * chunk:(c + 1) * chunk], preferred_element_type=F32), 0.0)
        hid = (hid * hid).astype(BF16)
        mlp = mlp + jnp.dot(hid, w2_ref[c * chunk:(c + 1) * chunk, :], preferred_element_type=F32)
    out_ref[0] = x1 + gt2_ref[0] * mlp


def _merge(x, o_att, yc, sga, sgb, gt1, sh2, sc2, gt2, g2, wa, wc, wo, w1, w2, tm):
    nb, t, d = x.shape
    nt = t // tm
    mod_rows = gt1.shape[1]
    mod_block = (1, tm, d) if mod_rows == t else (1, 1, d)
    mod_map = (lambda b, i: (b, i, 0)) if mod_rows == t else (lambda b, i: (b, 0, 0))
    row = lambda w: pl.BlockSpec((1, tm, w), lambda b, i: (b, i, 0))
    mod = pl.BlockSpec(mod_block, mod_map)
    const = lambda shape: _resident(shape, lambda b, i: (0,) * len(shape))
    w_bytes = 4 * (2 * ATT_WIDTH * d + d * d) + 2 * (2 * d * D_FF)
    est = w_bytes + 2 * tm * (4 * d + 2 * ATT_WIDTH * 2 + 2 * d * 2 + 4 * d) + 8 * tm * d * 4
    return pl.pallas_call(
        _merge_kernel,
        grid=(nb, nt),
        in_specs=[row(d), row(ATT_WIDTH), row(CONV_CH), row(d), row(d), mod, mod, mod, mod, const((1, d)),
                  const((ATT_WIDTH, d)), const((CONV_CH, d)), const((d, d)), const((d, D_FF)), const((D_FF, d))],
        out_specs=row(d),
        out_shape=jax.ShapeDtypeStruct((nb, t, d), F32),
        compiler_params=pltpu.CompilerParams(
            dimension_semantics=("arbitrary", "arbitrary"),
            vmem_limit_bytes=_vmem_limit(est)),
        name="merge_mlp",
    )(x, o_att, yc, sga, sgb, gt1, sh2, sc2, gt2, g2, wa, wc, wo, w1, w2)


def _neg_suffix_matrix(n):
    j = jnp.arange(n)[:, None]
    s = jnp.arange(n)[None, :]
    return jnp.where(j >= s, -1.0, 0.0).astype(BF16)


def _slot_minor_pages(cache):
    n_phys, slots, heads, dim = cache.shape
    return jnp.transpose(cache, (0, 2, 3, 1)).reshape(n_phys, heads * dim, slots)


def kernel(x_prompt, x_sample, c_prompt, c_sample, cache_k, cache_v, state_conv, page_table, rms_g1, rms_g2,
           w_ada, b_ada, w_in, q_norm_g, k_norm_g, sb_bias, conv_w, conv_b, w_att_out, w_conv_out, w_o, w_mlp1,
           w_mlp2):
    depth = w_in.shape[0]
    assert depth == 1, "single-layer step"
    l = 0
    bp, seq, d = x_prompt.shape
    bs, dec, _ = x_sample.shape
    assert dec * N_HEADS == SAMPLE_ROWS

    w_in_bf = w_in[l]
    wa, wc, wo = w_att_out[l], w_conv_out[l], w_o[l]
    w1, w2 = w_mlp1[l].astype(BF16), w_mlp2[l].astype(BF16)
    g1 = rms_g1[l].reshape(1, d)
    g2 = rms_g2[l].reshape(1, d)
    qg = jnp.tile(q_norm_g[l], N_HEADS).reshape(1, ATT_WIDTH)
    kg = jnp.tile(k_norm_g[l], N_HEADS).reshape(1, ATT_WIDTH)
    cw = conv_w[l]
    cb = conv_b[l].reshape(1, CONV_CH)
    grp = jnp.arange(V7X_MXU_DIM) // HEAD_DIM
    gmat = jnp.where(grp[:, None] == grp[None, :], 1.0 / HEAD_DIM, 0.0).astype(BF16)
    bias = sb_bias[l].astype(F32) * LOG2E

    mod = _ada(jnp.concatenate([c_prompt, c_sample], axis=0), w_ada[l], b_ada[l])
    mods = [mod[:, j * d:(j + 1) * d] for j in range(N_MOD)]
    mp = [m[:bp].reshape(bp, 1, d) for m in mods]
    ms_ = [jnp.repeat(m[bp:], dec, axis=0).reshape(1, bs * dec, d) for m in mods]

    (q_p, k_p, kb_p, v_p, vb_p, yc_p, sga_p, sgb_p, cs_p) = _proj(
        x_prompt, mp[0], mp[1], g1, w_in_bf, qg, kg, gmat, cw, cb)

    rows = bs * dec
    xs = x_sample.reshape(1, rows, d)
    st = state_conv[l]
    step = jnp.tile(jnp.arange(dec), bs)[:, None]
    flag1 = jnp.broadcast_to(step < 1, (rows, CONV_CH)).astype(F32)
    flag2 = jnp.broadcast_to(step < 2, (rows, CONV_CH)).astype(F32)
    zero = jnp.zeros((bs, dec - 2, CONV_CH), F32)
    val1 = jnp.concatenate([st[:, 1:2], zero, zero[:, :1]], axis=1).reshape(rows, CONV_CH)
    val2 = jnp.concatenate([st, zero], axis=1).reshape(rows, CONV_CH)
    hist = (jnp.stack([flag1, val1]), jnp.stack([flag2, val2]))
    (q_s, k_s, _, v_s, _, yc_s, sga_s, sgb_s, u_s) = _proj(
        xs, ms_[0], ms_[1], g1, w_in_bf, qg, kg, gmat, cw, cb, hist=hist)

    q4 = q_s.reshape(bs, dec, N_HEADS, HEAD_DIM)
    eye = jnp.eye(N_HEADS, dtype=BF16)
    qbd = jnp.einsum('bthd,hg->bthgd', q4, eye).reshape(bs, SAMPLE_ROWS, ATT_WIDTH)
    bcol = jnp.tile(bias, dec).reshape(SAMPLE_ROWS, 1)
    o_p, ot = _attention(q_p, kb_p, vb_p, bias, _neg_suffix_matrix(K_TILE),
                         page_table, qbd, bcol, _neg_suffix_matrix(PAGE_SIZE), k_s, v_s,
                         _slot_minor_pages(cache_k[l]), _slot_minor_pages(cache_v[l]))
    y_p = _merge(x_prompt, o_p, yc_p, sga_p, sgb_p, mp[2], mp[3], mp[4], mp[5], g2, wa, wc, wo, w1, w2,
                 tm=ROW_TILE)
    ot = ot[:, :, :SAMPLE_ROWS].reshape(bs, N_HEADS, HEAD_DIM, dec, N_HEADS)
    o_s = jnp.einsum('bhdth->bthd', ot).reshape(1, rows, ATT_WIDTH).astype(BF16)
    y_s = _merge(xs, o_s, yc_s, sga_s, sgb_s, ms_[2], ms_[3], ms_[4], ms_[5], g2, wa, wc, wo, w1, w2, tm=rows)

    def seq_major(xt, nb_, t_):
        x4 = xt.reshape(xt.shape[0], N_HEADS, HEAD_DIM, -1)
        return jnp.transpose(x4, (0, 3, 1, 2)).reshape(1, nb_, t_, N_HEADS, HEAD_DIM)

    return (y_p,
            y_s.reshape(bs, dec, d),
            seq_major(k_p, bp, seq), seq_major(v_p, bp, seq),
            cs_p.reshape(1, bp, CONV_WIDTH - 1, CONV_CH),
            seq_major(k_s, bs, dec), seq_major(v_s, bs, dec),
            u_s.reshape(bs, dec, CONV_CH)[:, dec - (CONV_WIDTH - 1):].reshape(1, bs, CONV_WIDTH - 1, CONV_CH))
```

```python
import functools
import math

import jax
import jax.numpy as jnp
from jax import lax
from jax.experimental import pallas as pl
from jax.experimental.pallas import tpu as pltpu

D_MODEL = 1024
N_HEADS = 8
HEAD_DIM = 64
ATT_WIDTH = N_HEADS * HEAD_DIM
CONV_CH = 512
CONV_WIDTH = 3
D_FF = 4 * D_MODEL
N_MOD = 6
PAGE_SIZE = 128
EPS = 1e-6
SB_SCALE = 1.0 / math.sqrt(HEAD_DIM)
LOG2E = math.log2(math.e)
Q_SCALE = SB_SCALE * LOG2E
IN_WIDTH = 3 * ATT_WIDTH + 3 * CONV_CH + 2 * D_MODEL

V7X_LANES = 128
V7X_SUBLANES = 8
V7X_MXU_DIM = 256
V7X_VMEM_BYTES = 64 * 1024 * 1024

ROW_TILE = 512
K_TILE = V7X_MXU_DIM
Q_TILE = K_TILE
PAGES_PER_STEP = 16
PAGE_BUFFERS = 2
HIST_ROWS = V7X_SUBLANES
SOFTPLUS2_LINEAR_FROM = 60.0
MASKED_LOG_WEIGHT = -1e30

F32 = jnp.float32
BF16 = jnp.bfloat16


def _vmem_limit(nbytes):
    return int(min(nbytes + (8 << 20), V7X_VMEM_BYTES - (6 << 20)))


def _resident(shape, index_map):
    return pl.BlockSpec(shape, index_map, pipeline_mode=pl.Buffered(1))


def _mm(lhs, rhs):
    return lax.dot_general(lhs, rhs, (((1,), (0,)), ((), ())), preferred_element_type=F32)


def _sigmoid(x):
    return 0.5 * jnp.tanh(0.5 * x) + 0.5


def _softplus2(z2):
    return jnp.log(1.0 + jnp.exp2(z2)) * LOG2E


def _ada_kernel(c_ref, w_ref, b_ref, o_ref):
    c = c_ref[...]
    s = (c * jax.nn.sigmoid(c)).astype(BF16)
    o_ref[...] = jnp.dot(s, w_ref[...], preferred_element_type=F32) + b_ref[...]


def _ada(c_all, w_ada, b_ada):
    n, d = c_all.shape
    width = w_ada.shape[1]
    tn = 1536
    return pl.pallas_call(
        _ada_kernel,
        grid=(width // tn,),
        in_specs=[pl.BlockSpec((n, d), lambda j: (0, 0)),
                  pl.BlockSpec((d, tn), lambda j: (0, j)),
                  pl.BlockSpec((1, tn), lambda j: (0, j))],
        out_specs=pl.BlockSpec((n, tn), lambda j: (0, j)),
        out_shape=jax.ShapeDtypeStruct((n, width), F32),
        compiler_params=pltpu.CompilerParams(
            dimension_semantics=("arbitrary",),
            vmem_limit_bytes=_vmem_limit(2 * d * tn * 4)),
        name="ada",
    )(c_all, w_ada, b_ada.reshape(1, width))


def _head_rms(p, gmat_ref, gain):
    sq = (p * p).astype(BF16)
    parts = []
    for c in range(ATT_WIDTH // V7X_MXU_DIM):
        sl = slice(c * V7X_MXU_DIM, (c + 1) * V7X_MXU_DIM)
        parts.append(jnp.dot(sq[:, sl], gmat_ref[...], preferred_element_type=F32))
    ms = jnp.concatenate(parts, axis=-1)
    return p * lax.rsqrt(ms + EPS) * gain


def _proj_kernel(*refs, tm, sample_mode):
    if sample_mode:
        (x_ref, sh_ref, sc_ref, g1_ref, w_ref, qg_ref, kg_ref, gmat_ref, cw_ref, cb_ref, h1_ref, h2_ref,
         q_out, k_out, kb_out, v_out, vb_out, yc_out, sga_out, sgb_out, u_out, conv_buf) = refs
    else:
        (x_ref, sh_ref, sc_ref, g1_ref, w_ref, qg_ref, kg_ref, gmat_ref, cw_ref, cb_ref,
         q_out, k_out, kb_out, v_out, vb_out, yc_out, sga_out, sgb_out, cs_out, conv_buf) = refs

        @pl.when((pl.program_id(0) == 0) & (pl.program_id(1) == 0))
        def _():
            conv_buf[0:HIST_ROWS, :] = jnp.zeros((HIST_ROWS, CONV_CH), F32)

    x = x_ref[0]
    ms = jnp.mean(x * x, axis=-1, keepdims=True)
    h = x * lax.rsqrt(ms + EPS) * g1_ref[...]
    h = (h * (1.0 + sc_ref[0]) + sh_ref[0]).astype(BF16)

    def part(lo, width):
        return jnp.dot(h, w_ref[:, lo:lo + width], preferred_element_type=F32)

    a = ATT_WIDTH
    c0 = 3 * a
    hc = part(c0, CONV_CH)
    bgate = part(c0 + CONV_CH, CONV_CH)
    cgate = part(c0 + 2 * CONV_CH, CONV_CH)
    u = cgate * hc

    g0 = c0 + 3 * CONV_CH
    sga_out[0] = _sigmoid(part(g0, D_MODEL)).astype(BF16)
    sgb_out[0] = _sigmoid(part(g0 + D_MODEL, D_MODEL)).astype(BF16)

    if sample_mode:
        u_out[0] = u
        prev = jnp.zeros((HIST_ROWS, CONV_CH), F32)
    else:
        first_tile = pl.program_id(1) == 0
        prev = jnp.where(first_tile, 0.0, conv_buf[0:HIST_ROWS, :])
    conv_buf[0:HIST_ROWS, :] = prev
    conv_buf[HIST_ROWS:HIST_ROWS + tm, :] = u
    u1 = conv_buf[HIST_ROWS - 1:HIST_ROWS - 1 + tm, :]
    u2 = conv_buf[HIST_ROWS - 2:HIST_ROWS - 2 + tm, :]
    if sample_mode:
        u1 = jnp.where(h1_ref[0] != 0, h1_ref[1], u1)
        u2 = jnp.where(h2_ref[0] != 0, h2_ref[1], u2)
    y = cb_ref[...] + cw_ref[0:1, :] * u2 + cw_ref[1:2, :] * u1 + cw_ref[2:3, :] * u
    yc_out[0] = (bgate * y).astype(BF16)
    if not sample_mode:
        conv_buf[0:HIST_ROWS, :] = u[tm - HIST_ROWS:, :]
        cs_out[0] = u[tm - (CONV_WIDTH - 1):, :]

    qn = _head_rms(part(0, a), gmat_ref, qg_ref[...])
    q_out[0] = (qn * Q_SCALE).astype(BF16)
    kn = _head_rms(part(a, a), gmat_ref, kg_ref[...])
    k_out[0] = kn.T
    kb_out[0] = kn.astype(BF16)
    v = part(2 * a, a)
    v_out[0] = v.T
    parity = (lax.broadcasted_iota(jnp.int32, v.shape, 1) // HEAD_DIM) % 2
    for p in range(2):
        vb_out[0, :, p * a:(p + 1) * a] = jnp.where(parity == p, v, 0.0).astype(BF16)


def _proj(x, sh, sc, g1, w_in_bf, qg, kg, gmat, conv_w, conv_b, hist=None):
    nb, t, d = x.shape
    sample_mode = hist is not None
    tm = t if sample_mode else ROW_TILE
    nt = t // tm
    mod_rows = sh.shape[1]
    mod_block = (1, tm, d) if mod_rows == t else (1, 1, d)
    mod_map = (lambda b, i: (b, i, 0)) if mod_rows == t else (lambda b, i: (b, 0, 0))
    row = lambda w: pl.BlockSpec((1, tm, w), lambda b, i: (b, i, 0))
    const = lambda shape: _resident(shape, lambda b, i: (0,) * len(shape))

    in_specs = [row(d), pl.BlockSpec(mod_block, mod_map), pl.BlockSpec(mod_block, mod_map),
                const((1, d)), const((d, IN_WIDTH)), const((1, ATT_WIDTH)), const((1, ATT_WIDTH)),
                const((V7X_MXU_DIM, V7X_MXU_DIM)), const((CONV_WIDTH, CONV_CH)), const((1, CONV_CH))]
    args = [x, sh, sc, g1, w_in_bf, qg, kg, gmat, conv_w, conv_b]
    out_shape = [jax.ShapeDtypeStruct((nb, t, ATT_WIDTH), BF16),
                 jax.ShapeDtypeStruct((nb, ATT_WIDTH, t), F32),
                 jax.ShapeDtypeStruct((nb, t, ATT_WIDTH), BF16),
                 jax.ShapeDtypeStruct((nb, ATT_WIDTH, t), F32),
                 jax.ShapeDtypeStruct((nb, t, 2 * ATT_WIDTH), BF16),
                 jax.ShapeDtypeStruct((nb, t, CONV_CH), BF16),
                 jax.ShapeDtypeStruct((nb, t, D_MODEL), BF16),
                 jax.ShapeDtypeStruct((nb, t, D_MODEL), BF16)]
    col = pl.BlockSpec((1, ATT_WIDTH, tm), lambda b, i: (b, 0, i))
    out_specs = [row(ATT_WIDTH), col, row(ATT_WIDTH), col,
                 row(2 * ATT_WIDTH), row(CONV_CH), row(D_MODEL), row(D_MODEL)]
    if sample_mode:
        in_specs += [const((2, t, CONV_CH)), const((2, t, CONV_CH))]
        args += list(hist)
        out_shape.append(jax.ShapeDtypeStruct((nb, t, CONV_CH), F32))
        out_specs.append(row(CONV_CH))
    else:
        out_shape.append(jax.ShapeDtypeStruct((nb, CONV_WIDTH - 1, CONV_CH), F32))
        out_specs.append(pl.BlockSpec((1, CONV_WIDTH - 1, CONV_CH), lambda b, i: (b, 0, 0)))

    est = (d * IN_WIDTH * 2 + 2 * tm * d * 4 + 2 * tm * (ATT_WIDTH * 16 + CONV_CH * 6 + D_MODEL * 4)
           + 6 * tm * D_MODEL * 4)
    return pl.pallas_call(
        functools.partial(_proj_kernel, tm=tm, sample_mode=sample_mode),
        grid=(nb, nt),
        in_specs=in_specs,
        out_specs=out_specs,
        out_shape=out_shape,
        scratch_shapes=[pltpu.VMEM((tm + HIST_ROWS, CONV_CH), F32)],
        compiler_params=pltpu.CompilerParams(
            dimension_semantics=("arbitrary", "arbitrary"),
            vmem_limit_bytes=_vmem_limit(est)),
        name="proj_sample" if sample_mode else "proj_prompt",
    )(*args)


def _prompt_attention_part(i, part, n_parts, bias_ref, q_ref, k_ref, v_ref, u_ref, o_ref,
                           qm_ref, z_buf, w_buf, carry_ref, acc_ref, side_work):
    tq = q_ref.shape[1]
    pair = V7X_LANES
    n_pairs = q_ref.shape[2] // pair
    per_pair = pair // HEAD_DIM

    def key_rows(j):
        return pl.ds(pl.multiple_of(j * K_TILE, K_TILE), K_TILE)

    pairs = range(n_pairs)

    def scores(j, slot, hps=pairs, hhs=range(per_pair)):
        for h in [hp * per_pair + hh for hp in hps for hh in hhs]:
            hp = h // per_pair
            kt = k_ref[0, key_rows(j), hp * pair:(hp + 1) * pair]
            z = lax.dot_general(qm_ref[h], kt, (((1,), (1,)), ((), ())), preferred_element_type=F32)
            z_buf[slot, h] = jnp.minimum(z + bias_ref[h], SOFTPLUS2_LINEAR_FROM)

    def log_weights(slot, diagonal, hps=pairs, hhs=range(per_pair)):
        for h in [hp * per_pair + hh for hp in hps for hh in hhs]:
            z = z_buf[slot, h]
            sp = _softplus2(z)
            if diagonal:
                row = lax.broadcasted_iota(jnp.int32, (tq, K_TILE), 0)
                col = lax.broadcasted_iota(jnp.int32, (tq, K_TILE), 1)
                valid = row > col
                sp = jnp.where(valid, sp, 0.0)
            p = _mm(sp, u_ref[...])
            carry = carry_ref[h]
            w = z + p + jnp.concatenate([carry] * (K_TILE // pair), axis=1)
            if diagonal:
                w = jnp.where(valid, w, MASKED_LOG_WEIGHT)
            w_buf[slot, h] = w
            carry_ref[h] = carry + jnp.broadcast_to(p[:, 0:1], carry.shape)

    def weighted_values(j, slot, hps=pairs):
        width = n_pairs * pair
        for hp in hps:
            out = None
            for hh in range(per_pair):
                vh = v_ref[0, key_rows(j), hh * width + hp * pair:hh * width + (hp + 1) * pair]
                contrib = _mm(jnp.exp2(w_buf[slot, hp * per_pair + hh]), vh)
                out = contrib if out is None else out + contrib
            acc_ref[hp] += out

    @pl.when((part > 0) & (part < n_parts - 1))
    def _():
        side_work()

    @pl.when(part == 0)
    def _():
        lane_q = lax.broadcasted_iota(jnp.int32, (tq, pair), 1)
        for hp in pairs:
            q2 = q_ref[0, :, hp * pair:(hp + 1) * pair]
            for hh in range(per_pair):
                qm_ref[hp * per_pair + hh] = jnp.where(lane_q // HEAD_DIM == hh, q2, jnp.zeros_like(q2))
        carry_ref[...] = jnp.zeros(carry_ref.shape, F32)
        acc_ref[...] = jnp.zeros(acc_ref.shape, F32)
        scores(i, 0)
        scores(jnp.maximum(i - 1, 0), 1)
        side_work()
        log_weights(0, diagonal=True)

    def step(n, slot):
        for hp in pairs:
            log_weights(1 - slot, False, [hp])
            scores(i - n - 2, slot, [hp])
            weighted_values(i - n, slot, [hp])

    def two_steps(m, _):
        step(2 * m, 0)
        step(2 * m + 1, 1)
        return 0

    n_full = jnp.maximum(i - 1, 0)
    n_loop = n_full // 2
    lax.fori_loop(n_loop * part // n_parts, n_loop * (part + 1) // n_parts, two_steps, 0)

    @pl.when(part == n_parts - 1)
    def _():
        @pl.when(n_full % 2 == 1)
        def _():
            step(n_full - 1, 0)

        @pl.when(i == 0)
        def _():
            side_work()

        @pl.when(i >= 1)
        def _():
            side_work()
            slot = (i - 1) % 2
            weighted_values(1, slot)
            log_weights(1 - slot, False)

        weighted_values(0, i % 2)
        for hp in pairs:
            o_ref[0, :, hp * pair:(hp + 1) * pair] = acc_ref[hp].astype(o_ref.dtype)


SAMPLE_ROWS = 32
QUAD = V7X_LANES // SAMPLE_ROWS


def _sample_attention_chunk(qbd_ref, bcol_ref, u_ref, knt_ref, vnt_ref, kt_refs, vt_refs, o_ref, acc_ref, carry_ref):
    n_pages = len(kt_refs)
    rows = SAMPLE_ROWS
    qbd = qbd_ref[0]
    bcol = bcol_ref[...]
    col_block = lax.broadcasted_iota(jnp.int32, (PAGE_SIZE, V7X_LANES), 1) // rows

    def scores(kt):
        z2 = jnp.dot(qbd, kt, preferred_element_type=F32) + bcol
        return jnp.minimum(z2, SOFTPLUS2_LINEAR_FROM)

    def suffix(sp):
        return jnp.dot(sp.astype(BF16), u_ref[...], preferred_element_type=F32)

    def transposed_weights(a_blocks):
        at = jnp.concatenate(a_blocks, axis=0).T
        return [jnp.where(col_block == p, at, 0.0) for p in range(len(a_blocks))]

    def new_keys(seq):
        n_q = rows // N_HEADS
        z = scores(knt_ref[0])
        key = lax.broadcasted_iota(jnp.int32, (rows, PAGE_SIZE), 1)
        qry = lax.broadcasted_iota(jnp.int32, (rows, PAGE_SIZE), 0) // N_HEADS
        valid = (key // n_q == seq) & (key % n_q < qry)
        sp = jnp.where(valid, _softplus2(z), 0.0)
        p = suffix(sp)
        a = jnp.where(valid, jnp.exp2(z + p), 0.0)
        zero = jnp.zeros_like(a)
        w = transposed_weights([a] + [zero] * (QUAD - 1))[0]
        acc_ref[...] = jnp.dot(vnt_ref[0], w, preferred_element_type=F32)
        carry_ref[...] = p[:, 0:1]

    def cached_pages():
        zs = [scores(kt_refs[g][...]) for g in range(n_pages)]
        z_all = jnp.concatenate(zs, axis=0)
        p_all = suffix(_softplus2(z_all))
        carry = carry_ref[...]
        carries = [None] * n_pages
        for g in reversed(range(n_pages)):
            carries[g] = carry
            carry = carry + p_all[g * rows:(g + 1) * rows, 0:1]
        carry_ref[...] = carry
        a_all = jnp.exp2(z_all + p_all + jnp.concatenate(carries, axis=0))

        acc = acc_ref[...]
        for q0 in range(0, n_pages, QUAD):
            w = transposed_weights([a_all[(q0 + p) * rows:(q0 + p + 1) * rows, :] for p in range(QUAD)])
            for p in range(0, QUAD, 2):
                vt2 = jnp.concatenate([vt_refs[q0 + p][...], vt_refs[q0 + p + 1][...]], axis=1)
                w2 = jnp.concatenate([w[p], w[p + 1]], axis=0)
                acc = acc + jnp.dot(vt2, w2, preferred_element_type=F32)
        acc_ref[...] = acc

    def finish():
        acc = acc_ref[...]
        total = acc
        for p in range(1, QUAD):
            total = total + pltpu.roll(acc, p * rows, axis=1)
        o_ref[0] = total

    return new_keys, cached_pages, finish


PROMPT_PARTS = 4


def _attention_kernel(order_ref, bias_ref, q_ref, k_ref, v_ref, up_ref, qbd_ref, bcol_ref, us_ref, knt_ref, vnt_ref,
                      ck_hbm, cv_hbm, o_ref, os_ref, qm_ref, z_buf, w_buf, carry_ref, acc_ref, s_acc_ref,
                      s_carry_ref, page_buf, page_sem, *, n_pages, n_chunks):
    i = pl.program_id(1)
    part = pl.program_id(2)
    n_steps = pl.num_programs(0) * pl.num_programs(1) * PROMPT_PARTS
    step = (pl.program_id(0) * pl.num_programs(1) + i) * PROMPT_PARTS + part
    half = step % 2

    def page_copies(s, into):
        copies = []
        for g in range(n_pages):
            page = order_ref[s, g]
            copies.append(pltpu.make_async_copy(ck_hbm.at[page], page_buf.at[into, 0, g], page_sem.at[into]))
            copies.append(pltpu.make_async_copy(cv_hbm.at[page], page_buf.at[into, 1, g], page_sem.at[into]))
        return copies

    @pl.when(step == 0)
    def _():
        for n, copy in enumerate(page_copies(0, 0)):
            copy.start(priority=n % 2)

    @pl.when(step + 1 < n_steps)
    def _():
        for n, copy in enumerate(page_copies(step + 1, 1 - half)):
            copy.start(priority=n % 2)

    for copy in page_copies(step, half):
        copy.wait()
    kt_refs = [page_buf.at[half, 0, g] for g in range(n_pages)]
    vt_refs = [page_buf.at[half, 1, g] for g in range(n_pages)]
    new_keys, cached_pages, finish = _sample_attention_chunk(
        qbd_ref, bcol_ref, us_ref, knt_ref, vnt_ref, kt_refs, vt_refs, os_ref, s_acc_ref, s_carry_ref)
    chunk = step % n_chunks
    pl.when(chunk == 0)(functools.partial(new_keys, step // n_chunks))
    _prompt_attention_part(i, part, PROMPT_PARTS, bias_ref, q_ref, k_ref, v_ref, up_ref, o_ref,
                           qm_ref, z_buf, w_buf, carry_ref, acc_ref, cached_pages)
    pl.when(chunk == n_chunks - 1)(finish)


def _attention(q_bf, k_bf, v_bf2, bias, u_prompt, page_table, qbd, bcol, u_sample, knt, vnt, cache_kt, cache_vt):
    nb, t, width = q_bf.shape
    bs, n_tab = page_table.shape
    tq = Q_TILE
    nq = t // tq
    pair = V7X_LANES
    g = PAGES_PER_STEP
    n_chunks = n_tab // g
    rows = SAMPLE_ROWS
    assert qbd.shape[1] == rows and g % QUAD == 0
    assert nb * nq * PROMPT_PARTS == bs * n_chunks, "one page chunk per grid step"
    assert knt.shape == vnt.shape == (1, ATT_WIDTH, PAGE_SIZE), "new keys of all sequences: one page of columns"
    page_block = (1, ATT_WIDTH, PAGE_SIZE)
    page_order = page_table.reshape(bs, n_chunks, g)[:, ::-1, :].reshape(bs * n_chunks, g)

    def seq_of(b, i, c):
        return ((b * nq + i) * PROMPT_PARTS + c) // n_chunks

    per_seq = lambda shape: pl.BlockSpec(shape, lambda b, i, c, po, bs_: (seq_of(b, i, c), 0, 0))
    const = lambda shape: pl.BlockSpec(shape, lambda b, i, c, po, bs_: (0,) * len(shape))
    in_specs = [pl.BlockSpec((1, tq, width), lambda b, i, c, po, bs_: (b, i, 0)),
                _resident((1, t, width), lambda b, i, c, po, bs_: (b, 0, 0)),
                _resident((1, t, 2 * width), lambda b, i, c, po, bs_: (b, 0, 0)),
                _resident((K_TILE, K_TILE), lambda b, i, c, po, bs_: (0, 0)),
                per_seq((1, rows, ATT_WIDTH)), const((rows, 1)), const((PAGE_SIZE, PAGE_SIZE)),
                const(page_block), const(page_block),
                pl.BlockSpec(memory_space=pl.ANY), pl.BlockSpec(memory_space=pl.ANY)]
    tile_bytes = tq * K_TILE * 4
    scratch = [pltpu.VMEM((N_HEADS, tq, pair), BF16),
               pltpu.VMEM((2, N_HEADS, tq, K_TILE), F32),
               pltpu.VMEM((2, N_HEADS, tq, K_TILE), F32),
               pltpu.VMEM((N_HEADS, tq, pair), F32),
               pltpu.VMEM((width // pair, tq, pair), F32),
               pltpu.VMEM((ATT_WIDTH, V7X_LANES), F32),
               pltpu.VMEM((rows, 1), F32),
               pltpu.VMEM((PAGE_BUFFERS, 2, g, ATT_WIDTH, PAGE_SIZE), F32),
               pltpu.SemaphoreType.DMA((PAGE_BUFFERS,))]
    grid_spec = pltpu.PrefetchScalarGridSpec(
        num_scalar_prefetch=2,
        grid=(nb, nq, PROMPT_PARTS),
        in_specs=in_specs,
        out_specs=[pl.BlockSpec((1, tq, width), lambda b, i, c, po, bs_: (b, i, 0)),
                   pl.BlockSpec((1, ATT_WIDTH, V7X_LANES), lambda b, i, c, po, bs_: (seq_of(b, i, c), 0, 0))],
        scratch_shapes=scratch)
    est = (3 * t * width * 2 + 4 * tq * width * 2 + 4 * N_HEADS * tile_bytes
           + N_HEADS * tq * pair * (2 + 4) + 4 * tq * pair * 4 + 8 * tile_bytes
           + PAGE_BUFFERS * 2 * g * PAGE_SIZE * ATT_WIDTH * 4 + 6 * PAGE_SIZE * ATT_WIDTH * 4)
    return pl.pallas_call(
        functools.partial(_attention_kernel, n_pages=g, n_chunks=n_chunks),
        grid_spec=grid_spec,
        out_shape=[jax.ShapeDtypeStruct((nb, t, width), BF16),
                   jax.ShapeDtypeStruct((bs, ATT_WIDTH, V7X_LANES), F32)],
        compiler_params=pltpu.CompilerParams(
            dimension_semantics=("arbitrary", "arbitrary", "arbitrary"),
            vmem_limit_bytes=_vmem_limit(est)),
        name="attention",
    )(page_order, bias, q_bf, k_bf, v_bf2, u_prompt, qbd, bcol, u_sample, knt, vnt, cache_kt, cache_vt)


def _merge_kernel(x_ref, o_ref, yc_ref, sga_ref, sgb_ref, gt1_ref, sh2_ref, sc2_ref, gt2_ref, g2_ref,
                  wa_ref, wc_ref, wo_ref, w1_ref, w2_ref, out_ref):
    ya = jnp.dot(o_ref[0], wa_ref[...], preferred_element_type=F32)
    yb = jnp.dot(yc_ref[0], wc_ref[...], preferred_element_type=F32)
    mixed = (sga_ref[0].astype(F32) * ya + sgb_ref[0].astype(F32) * yb).astype(BF16)
    x1 = x_ref[0] + gt1_ref[0] * jnp.dot(mixed, wo_ref[...], preferred_element_type=F32)
    ms = jnp.mean(x1 * x1, axis=-1, keepdims=True)
    h2 = x1 * lax.rsqrt(ms + EPS) * g2_ref[...]
    h2 = (h2 * (1.0 + sc2_ref[0]) + sh2_ref[0]).astype(BF16)
    chunk = D_MODEL
    mlp = jnp.zeros(x1.shape, F32)
    for c in range(D_FF // chunk):
        hid = jnp.maximum(jnp.dot(h2, w1_ref[:, c * chunk:(c + 1) * chunk], preferred_element_type=F32), 0.0)
        hid = (hid * hid).astype(BF16)
        mlp = mlp + jnp.dot(hid, w2_ref[c * chunk:(c + 1) * chunk, :], preferred_element_type=F32)
    out_ref[0] = x1 + gt2_ref[0] * mlp


def _merge(x, o_att, yc, sga, sgb, gt1, sh2, sc2, gt2, g2, wa, wc, wo, w1, w2, tm):
    nb, t, d = x.shape
    nt = t // tm
    mod_rows = gt1.shape[1]
    mod_block = (1, tm, d) if mod_rows == t else (1, 1, d)
    mod_map = (lambda b, i: (b, i, 0)) if mod_rows == t else (lambda b, i: (b, 0, 0))
    row = lambda w: pl.BlockSpec((1, tm, w), lambda b, i: (b, i, 0))
    mod = pl.BlockSpec(mod_block, mod_map)
    const = lambda shape: _resident(shape, lambda b, i: (0,) * len(shape))
    w_bytes = 2 * (2 * ATT_WIDTH * d + d * d + 2 * d * D_FF)
    est = w_bytes + 2 * tm * (4 * d + 2 * ATT_WIDTH * 2 + 2 * d * 2 + 4 * d) + 8 * tm * d * 4
    return pl.pallas_call(
        _merge_kernel,
        grid=(nb, nt),
        in_specs=[row(d), row(ATT_WIDTH), row(CONV_CH), row(d), row(d), mod, mod, mod, mod, const((1, d)),
                  const((ATT_WIDTH, d)), const((CONV_CH, d)), const((d, d)), const((d, D_FF)), const((D_FF, d))],
        out_specs=row(d),
        out_shape=jax.ShapeDtypeStruct((nb, t, d), F32),
        compiler_params=pltpu.CompilerParams(
            dimension_semantics=("arbitrary", "arbitrary"),
            vmem_limit_bytes=_vmem_limit(est)),
        name="merge_mlp",
    )(x, o_att, yc, sga, sgb, gt1, sh2, sc2, gt2, g2, wa, wc, wo, w1, w2)


def _neg_suffix_matrix(n):
    j = jnp.arange(n)[:, None]
    s = jnp.arange(n)[None, :]
    return jnp.where(j >= s, -1.0, 0.0).astype(BF16)


def _slot_minor_pages(cache):
    n_phys, slots, heads, dim = cache.shape
    return jnp.transpose(cache, (0, 2, 3, 1)).reshape(n_phys, heads * dim, slots)


def kernel(x_prompt, x_sample, c_prompt, c_sample, cache_k, cache_v, state_conv, page_table, rms_g1, rms_g2,
           w_ada, b_ada, w_in, q_norm_g, k_norm_g, sb_bias, conv_w, conv_b, w_att_out, w_conv_out, w_o, w_mlp1,
           w_mlp2):
    depth = w_in.shape[0]
    assert depth == 1, "single-layer step"
    l = 0
    bp, seq, d = x_prompt.shape
    bs, dec, _ = x_sample.shape
    assert dec * N_HEADS == SAMPLE_ROWS

    w_in_bf = w_in[l].astype(BF16)
    wa, wc, wo = w_att_out[l].astype(BF16), w_conv_out[l].astype(BF16), w_o[l].astype(BF16)
    w1, w2 = w_mlp1[l].astype(BF16), w_mlp2[l].astype(BF16)
    g1 = rms_g1[l].reshape(1, d)
    g2 = rms_g2[l].reshape(1, d)
    qg = jnp.tile(q_norm_g[l], N_HEADS).reshape(1, ATT_WIDTH)
    kg = jnp.tile(k_norm_g[l], N_HEADS).reshape(1, ATT_WIDTH)
    cw = conv_w[l]
    cb = conv_b[l].reshape(1, CONV_CH)
    grp = jnp.arange(V7X_MXU_DIM) // HEAD_DIM
    gmat = jnp.where(grp[:, None] == grp[None, :], 1.0 / HEAD_DIM, 0.0).astype(BF16)
    bias = sb_bias[l].astype(F32) * LOG2E

    mod = _ada(jnp.concatenate([c_prompt, c_sample], axis=0), w_ada[l], b_ada[l])
    mods = [mod[:, j * d:(j + 1) * d] for j in range(N_MOD)]
    mp = [m[:bp].reshape(bp, 1, d) for m in mods]
    ms_ = [jnp.repeat(m[bp:], dec, axis=0).reshape(1, bs * dec, d) for m in mods]

    (q_p, k_p, kb_p, v_p, vb_p, yc_p, sga_p, sgb_p, cs_p) = _proj(
        x_prompt, mp[0], mp[1], g1, w_in_bf, qg, kg, gmat, cw, cb)

    rows = bs * dec
    xs = x_sample.reshape(1, rows, d)
    st = state_conv[l]
    step = jnp.tile(jnp.arange(dec), bs)[:, None]
    flag1 = jnp.broadcast_to(step < 1, (rows, CONV_CH)).astype(F32)
    flag2 = jnp.broadcast_to(step < 2, (rows, CONV_CH)).astype(F32)
    zero = jnp.zeros((bs, dec - 2, CONV_CH), F32)
    val1 = jnp.concatenate([st[:, 1:2], zero, zero[:, :1]], axis=1).reshape(rows, CONV_CH)
    val2 = jnp.concatenate([st, zero], axis=1).reshape(rows, CONV_CH)
    hist = (jnp.stack([flag1, val1]), jnp.stack([flag2, val2]))
    (q_s, k_s, _, v_s, _, yc_s, sga_s, sgb_s, u_s) = _proj(
        xs, ms_[0], ms_[1], g1, w_in_bf, qg, kg, gmat, cw, cb, hist=hist)

    q4 = q_s.reshape(bs, dec, N_HEADS, HEAD_DIM)
    eye = jnp.eye(N_HEADS, dtype=BF16)
    qbd = jnp.einsum('bthd,hg->bthgd', q4, eye).reshape(bs, SAMPLE_ROWS, ATT_WIDTH)
    bcol = jnp.tile(bias, dec).reshape(SAMPLE_ROWS, 1)
    o_p, ot = _attention(q_p, kb_p, vb_p, bias, _neg_suffix_matrix(K_TILE),
                         page_table, qbd, bcol, _neg_suffix_matrix(PAGE_SIZE), k_s, v_s,
                         _slot_minor_pages(cache_k[l]), _slot_minor_pages(cache_v[l]))
    y_p = _merge(x_prompt, o_p, yc_p, sga_p, sgb_p, mp[2], mp[3], mp[4], mp[5], g2, wa, wc, wo, w1, w2,
                 tm=ROW_TILE)
    ot = ot[:, :, :SAMPLE_ROWS].reshape(bs, N_HEADS, HEAD_DIM, dec, N_HEADS)
    o_s = jnp.einsum('bhdth->bthd', ot).reshape(1, rows, ATT_WIDTH).astype(BF16)
    y_s = _merge(xs, o_s, yc_s, sga_s, sgb_s, ms_[2], ms_[3], ms_[4], ms_[5], g2, wa, wc, wo, w1, w2, tm=rows)

    def seq_major(xt, nb_, t_):
        x4 = xt.reshape(xt.shape[0], N_HEADS, HEAD_DIM, -1)
        return jnp.transpose(x4, (0, 3, 1, 2)).reshape(1, nb_, t_, N_HEADS, HEAD_DIM)

    return (y_p,
            y_s.reshape(bs, dec, d),
            seq_major(k_p, bp, seq), seq_major(v_p, bp, seq),
            cs_p.reshape(1, bp, CONV_WIDTH - 1, CONV_CH),
            seq_major(k_s, bs, dec), seq_major(v_s, bs, dec),
            u_s.reshape(bs, dec, CONV_CH)[:, dec - (CONV_WIDTH - 1):].reshape(1, bs, CONV_WIDTH - 1, CONV_CH))
```
